```python
import jax, jax.numpy as jnp
from jax import lax
import numpy as np

D_MODEL = 4096
BATCH = 2
SEQ = 4096
DEPTH = 4
DEC_BATCH = 4
DEC_SEQ = 4096
PAST_LEN = 128

GRID_W = 64
Q_BLOCK = 128
ROPE_THETA = 10000.0
NORM_EPS = 1e-6
LN_EPS = 1e-5
H_A = 16
Q_LORA = 1024
KV_LORA = 512
NOPE_A = 128
ROPE_A = 64
V_A = 128
H_B = 16
KV_B = 4
HD_B = 128
D_FF_DENSE = 11008
D_FF_EXPERT = 4096
N_EXPERTS = 8
TOP_K = 2
N_DENSE = (DEPTH + 1) // 2
N_MOE = DEPTH // 2
DN_ALPHA = (2 * DEPTH) ** 0.25
DN_BETA = (8 * DEPTH) ** -0.25
Q_LAT_END = Q_LORA
KV_LAT_END = Q_LAT_END + KV_LORA
K_ROPE_END = KV_LAT_END + ROPE_A
Q_B_END = K_ROPE_END + H_B * HD_B
K_B_END = Q_B_END + KV_B * HD_B
V_B_END = K_B_END + KV_B * HD_B
G_A_END = V_B_END + D_MODEL
W_IN_COLS = G_A_END + D_MODEL
IN_SPLITS = (Q_LAT_END, KV_LAT_END, K_ROPE_END, Q_B_END, K_B_END, V_B_END, G_A_END)

kernel_name = 'hybrid_mla_gqa_axial_deepnorm_encoder'


def _rmsnorm(x, g):
    xf = x.astype(jnp.float32)
    y = xf * lax.rsqrt(jnp.mean(xf * xf, axis=-1, keepdims=True) + NORM_EPS)
    return (y * g.astype(jnp.float32)).astype(x.dtype)


def _layernorm(x, g, b):
    xf = x.astype(jnp.float32)
    mu = jnp.mean(xf, axis=-1, keepdims=True)
    var = jnp.mean(jnp.square(xf - mu), axis=-1, keepdims=True)
    y = (xf - mu) * lax.rsqrt(var + LN_EPS)
    return (y * g.astype(jnp.float32) + b.astype(jnp.float32)).astype(x.dtype)


def _softmax_f32(s, scale, dtype):
    return jax.nn.softmax(s.astype(jnp.float32) * scale, axis=-1).astype(dtype)


def _axial_rope_tables(t_len, dim):
    seg = dim // 2
    inv = ROPE_THETA ** (-jnp.arange(0, seg, 2, dtype=jnp.float32) / seg)
    t = jnp.arange(t_len)
    row = (t // GRID_W).astype(jnp.float32)
    col = (t % GRID_W).astype(jnp.float32)
    ar = row[:, None] * inv[None, :]
    ac = col[:, None] * inv[None, :]
    ar = jnp.concatenate([ar, ar], axis=-1)
    ac = jnp.concatenate([ac, ac], axis=-1)
    return (jnp.cos(ar), jnp.sin(ar), jnp.cos(ac), jnp.sin(ac))


def _rotate(x, c, s):
    x1, x2 = jnp.split(x, 2, axis=-1)
    return x * c + jnp.concatenate([-x2, x1], axis=-1) * s


def _apply_axial_rope(x, tabs):
    cr, sr, cc, sc = tabs
    xf = x.astype(jnp.float32)
    h = x.shape[-1] // 2
    out = jnp.concatenate([_rotate(xf[..., :h], cr, sr), _rotate(xf[..., h:], cc, sc)], axis=-1)
    return out.astype(x.dtype)


def _heads(tabs):
    return tuple(t[:, None, :] for t in tabs)


def _sweep_query_blocks(fn, *qs):
    b, t = qs[0].shape[:2]
    nb = t // Q_BLOCK
    blocks = tuple(jnp.moveaxis(q.reshape((b, nb, Q_BLOCK) + q.shape[2:]), 1, 0) for q in qs)
    out = lax.map(lambda blk: fn(*blk), blocks)
    return jnp.moveaxis(out, 0, 1).reshape((b, t) + out.shape[3:])


def _mla_attention(q_nope, q_rope, k_nope, k_rope, v):
    scale = (NOPE_A + ROPE_A) ** -0.5

    def block(qn, qr):
        s = jnp.einsum('bqhd,bkhd->bhqk', qn, k_nope) + jnp.einsum('bqhr,bkr->bhqk', qr, k_rope)
        p = _softmax_f32(s, scale, v.dtype)
        return jnp.einsum('bhqk,bkhd->bqhd', p, v)

    return _sweep_query_blocks(block, q_nope, q_rope)


def _gqa_attention(q, k, v):
    scale = HD_B ** -0.5

    def block(qb):
        s = jnp.einsum('bqghd,bkgd->bghqk', qb, k)
        p = _softmax_f32(s, scale, v.dtype)
        return jnp.einsum('bghqk,bkgd->bqghd', p, v)

    return _sweep_query_blocks(block, q)


def _mixer(x, tabs_a, tabs_b, w_in, gate_bias, q_a_norm, w_q_b, kv_a_norm, w_kv_b, q_norm, k_norm, w_br_a, w_br_b, w_o):
    b, t, _ = x.shape
    h = jnp.einsum('btd,dc->btc', x, w_in)
    q_lat, kv_lat, k_rope, q_b, k_b, v_b, g_a, g_b = jnp.split(h, IN_SPLITS, axis=-1)
    qa = jnp.einsum('btr,rc->btc', _rmsnorm(q_lat, q_a_norm), w_q_b).reshape(b, t, H_A, NOPE_A + ROPE_A)
    qa_nope, qa_rope = qa[..., :NOPE_A], _apply_axial_rope(qa[..., NOPE_A:], _heads(tabs_a))
    kv = jnp.einsum('btr,rc->btc', _rmsnorm(kv_lat, kv_a_norm), w_kv_b).reshape(b, t, H_A, NOPE_A + V_A)
    ka_nope, va = kv[..., :NOPE_A], kv[..., NOPE_A:]
    ka_rope = _apply_axial_rope(k_rope, tabs_a)
    o_a = _mla_attention(qa_nope, qa_rope, ka_nope, ka_rope, va).reshape(b, t, H_A * V_A)
    qb = _apply_axial_rope(_rmsnorm(q_b.reshape(b, t, H_B, HD_B), q_norm), _heads(tabs_b))
    kb = _apply_axial_rope(_rmsnorm(k_b.reshape(b, t, KV_B, HD_B), k_norm), _heads(tabs_b))
    vb = v_b.reshape(b, t, KV_B, HD_B)
    o_b = _gqa_attention(qb.reshape(b, t, KV_B, H_B // KV_B, HD_B), kb, vb).reshape(b, t, H_B * HD_B)
    y = (jax.nn.sigmoid(g_a + gate_bias[0]) * jnp.einsum('btc,cd->btd', o_a, w_br_a)
         + jax.nn.sigmoid(g_b + gate_bias[1]) * jnp.einsum('btc,cd->btd', o_b, w_br_b))
    return jnp.einsum('btd,de->bte', y, w_o)


def _dense_swiglu(x, w_gate, w_up, w_down):
    hdn = jax.nn.silu(jnp.einsum('btd,df->btf', x, w_gate)) * jnp.einsum('btd,df->btf', x, w_up)
    return jnp.einsum('btf,fd->btd', hdn, w_down)


def _moe_swiglu(x, w_router, w_gate, w_up, w_down):
    b, t, d = x.shape
    xt = x.reshape(b * t, d)
    logits = jnp.einsum('nd,de->ne', xt, w_router).astype(jnp.float32)
    top_v, top_i = lax.top_k(logits, TOP_K)
    top_w = jax.nn.softmax(top_v, axis=-1)
    combine = jnp.sum(jax.nn.one_hot(top_i, N_EXPERTS, dtype=jnp.float32) * top_w[..., None], axis=1).astype(x.dtype)
    out = jnp.zeros_like(xt)
    for e in range(N_EXPERTS):
        hdn = jax.nn.silu(xt @ w_gate[e]) * (xt @ w_up[e])
        out = out + combine[:, e:e + 1] * (hdn @ w_down[e])
    return out.reshape(b, t, d)


def _layer_stack(x, w_in, gate_bias, q_a_norm, w_q_b, kv_a_norm, w_kv_b, q_norm, k_norm, w_br_a, w_br_b, w_o,
                 ln1_g, ln1_b, ffn_w_gate, ffn_w_up, ffn_w_down, moe_w_router, moe_w_gate, moe_w_up, moe_w_down,
                 ln2_g, ln2_b):
    t_len = x.shape[1]
    tabs_a = _axial_rope_tables(t_len, ROPE_A)
    tabs_b = _axial_rope_tables(t_len, HD_B)
    for l in range(DEPTH):
        mix = _mixer(x, tabs_a, tabs_b, w_in[l], gate_bias[l], q_a_norm[l], w_q_b[l], kv_a_norm[l], w_kv_b[l],
                     q_norm[l], k_norm[l], w_br_a[l], w_br_b[l], w_o[l])
        x = _layernorm(DN_ALPHA * x + mix, ln1_g[l], ln1_b[l])
        i = l // 2
        if l % 2 == 0:
            f = _dense_swiglu(x, ffn_w_gate[i], ffn_w_up[i], ffn_w_down[i])
        else:
            f = _moe_swiglu(x, moe_w_router[i], moe_w_gate[i], moe_w_up[i], moe_w_down[i])
        x = _layernorm(DN_ALPHA * x + f, ln2_g[l], ln2_b[l])
    return x


def setup_inputs(seed: int = 0) -> dict:
    key = jax.random.key(seed)
    ks = jax.random.split(key, 24)

    def nrm(k, shape, scale):
        return jax.random.normal(k, shape, jnp.float32) * scale

    return {
        'x_prompt': nrm(ks[0], (BATCH, SEQ, D_MODEL), 1.0),
        'x_sample': nrm(ks[1], (DEC_BATCH, DEC_SEQ, D_MODEL), 1.0),
        'w_in': nrm(ks[2], (DEPTH, D_MODEL, W_IN_COLS), D_MODEL ** -0.5),
        'gate_bias': nrm(ks[3], (DEPTH, 2, D_MODEL), 0.02),
        'q_a_norm': 1.0 + nrm(ks[4], (DEPTH, Q_LORA), 0.02),
        'w_q_b': nrm(ks[5], (DEPTH, Q_LORA, H_A * (NOPE_A + ROPE_A)), Q_LORA ** -0.5),
        'kv_a_norm': 1.0 + nrm(ks[6], (DEPTH, KV_LORA), 0.02),
        'w_kv_b': nrm(ks[7], (DEPTH, KV_LORA, H_A * (NOPE_A + V_A)), KV_LORA ** -0.5),
        'q_norm': 1.0 + nrm(ks[8], (DEPTH, HD_B), 0.02),
        'k_norm': 1.0 + nrm(ks[9], (DEPTH, HD_B), 0.02),
        'w_br_a': nrm(ks[10], (DEPTH, H_A * V_A, D_MODEL), (H_A * V_A) ** -0.5),
        'w_br_b': nrm(ks[11], (DEPTH, H_B * HD_B, D_MODEL), (H_B * HD_B) ** -0.5),
        'w_o': nrm(ks[12], (DEPTH, D_MODEL, D_MODEL), DN_BETA * D_MODEL ** -0.5),
        'ln1_g': 1.0 + nrm(ks[13], (DEPTH, D_MODEL), 0.02),
        'ln1_b': nrm(ks[14], (DEPTH, D_MODEL), 0.02),
        'ffn_w_gate': nrm(ks[15], (N_DENSE, D_MODEL, D_FF_DENSE), D_MODEL ** -0.5),
        'ffn_w_up': nrm(ks[16], (N_DENSE, D_MODEL, D_FF_DENSE), D_MODEL ** -0.5),
        'ffn_w_down': nrm(ks[17], (N_DENSE, D_FF_DENSE, D_MODEL), DN_BETA * D_FF_DENSE ** -0.5),
        'moe_w_router': nrm(ks[18], (N_MOE, D_MODEL, N_EXPERTS), D_MODEL ** -0.5),
        'moe_w_gate': nrm(ks[19], (N_MOE, N_EXPERTS, D_MODEL, D_FF_EXPERT), D_MODEL ** -0.5),
        'moe_w_up': nrm(ks[20], (N_MOE, N_EXPERTS, D_MODEL, D_FF_EXPERT), D_MODEL ** -0.5),
        'moe_w_down': nrm(ks[21], (N_MOE, N_EXPERTS, D_FF_EXPERT, D_MODEL), DN_BETA * D_FF_EXPERT ** -0.5),
        'ln2_g': 1.0 + nrm(ks[22], (DEPTH, D_MODEL), 0.02),
        'ln2_b': nrm(ks[23], (DEPTH, D_MODEL), 0.02),
    }


def reference(x_prompt, x_sample, w_in, gate_bias, q_a_norm, w_q_b, kv_a_norm, w_kv_b, q_norm, k_norm, w_br_a,
              w_br_b, w_o, ln1_g, ln1_b, ffn_w_gate, ffn_w_up, ffn_w_down, moe_w_router, moe_w_gate, moe_w_up,
              moe_w_down, ln2_g, ln2_b):
    y_prompt = _layer_stack(x_prompt, w_in, gate_bias, q_a_norm, w_q_b, kv_a_norm, w_kv_b, q_norm, k_norm, w_br_a,
                            w_br_b, w_o, ln1_g, ln1_b, ffn_w_gate, ffn_w_up, ffn_w_down, moe_w_router, moe_w_gate,
                            moe_w_up, moe_w_down, ln2_g, ln2_b)
    y_sample = _layer_stack(x_sample, w_in, gate_bias, q_a_norm, w_q_b, kv_a_norm, w_kv_b, q_norm, k_norm, w_br_a,
                            w_br_b, w_o, ln1_g, ln1_b, ffn_w_gate, ffn_w_up, ffn_w_down, moe_w_router, moe_w_gate,
                            moe_w_up, moe_w_down, ln2_g, ln2_b)
    return (y_prompt, y_sample)
```

```python
import functools

import jax
import jax.numpy as jnp
from jax import lax
from jax.experimental import pallas as pl
from jax.experimental.pallas import tpu as pltpu

F32 = jnp.float32
BF16 = jnp.bfloat16

D_MODEL = 4096
SEQ_LEN = 4096
DEPTH = 4
GRID_W = 64
ROPE_THETA = 10000.0
NORM_EPS = 1e-6
LN_EPS = 1e-5
H_A, Q_LORA, KV_LORA, NOPE_A, ROPE_A, V_A = 16, 1024, 512, 128, 64, 128
H_B, KV_B, HD_B = 16, 4, 128
D_FF_DENSE = 11008
D_FF_EXPERT = 4096
N_EXPERTS = 8
DN_ALPHA = (2 * DEPTH) ** 0.25

LANES = 128
MXU_DIM = 256
VMEM_BUDGET = 56 * 1024 * 1024

DOWN_BK = 1024
HEAD_A_PAD = 2 * LANES


def _cparams(vmem_bytes):
    return pltpu.CompilerParams(vmem_limit_bytes=int(min(VMEM_BUDGET, vmem_bytes)))


def _dot(a, b):
    return jnp.dot(a, b, preferred_element_type=F32)


def _rope_lanes(y, cos, sin_signed, half):
    lane = lax.broadcasted_iota(jnp.int32, y.shape, y.ndim - 1)
    lo = (lane % (2 * half)) < half
    partner = jnp.where(lo, pltpu.roll(y, LANES - half, y.ndim - 1), pltpu.roll(y, half, y.ndim - 1))
    return y * cos + partner * sin_signed


def _mm_plain_kernel(x_ref, w_ref, o_ref):
    o_ref[...] = _dot(x_ref[...], w_ref[...]).astype(o_ref.dtype)


def _mm_plain(x, w, out_dtype, bm, bn):
    m, k = x.shape
    n = w.shape[1]
    ob = jnp.dtype(out_dtype).itemsize
    vmem = 2 * (bm * k * 2 + k * bn * 2 + bm * bn * ob) + bm * bn * 4 + (4 << 20)
    return pl.pallas_call(
        _mm_plain_kernel,
        grid=(m // bm, n // bn),
        in_specs=[pl.BlockSpec((bm, k), lambda i, j: (i, 0)),
                  pl.BlockSpec((k, bn), lambda i, j: (0, j))],
        out_specs=pl.BlockSpec((bm, bn), lambda i, j: (i, j)),
        out_shape=jax.ShapeDtypeStruct((m, n), out_dtype),
        compiler_params=_cparams(vmem),
    )(x, w)


def _mm_rms_kernel(x_ref, w_ref, g_ref, o_ref):
    h = _dot(x_ref[...], w_ref[...])
    y = h * lax.rsqrt(jnp.mean(h * h, axis=-1, keepdims=True) + NORM_EPS)
    o_ref[...] = (y * g_ref[...]).astype(o_ref.dtype)


def _mm_rms(x, w, g, bm):
    m, k = x.shape
    n = w.shape[1]
    vmem = 2 * (bm * k * 2 + k * n * 2 + bm * n * 2) + 2 * bm * n * 4 + (4 << 20)
    return pl.pallas_call(
        _mm_rms_kernel,
        grid=(m // bm,),
        in_specs=[pl.BlockSpec((bm, k), lambda i: (i, 0)),
                  pl.BlockSpec((k, n), lambda i: (0, 0)),
                  pl.BlockSpec((1, n), lambda i: (0, 0))],
        out_specs=pl.BlockSpec((bm, n), lambda i: (i, 0)),
        out_shape=jax.ShapeDtypeStruct((m, n), BF16),
        compiler_params=_cparams(vmem),
    )(x, w, g.reshape(1, n).astype(F32))


def _mm_rope_kernel(x_ref, w_ref, cos_ref, sin_ref, o_ref, *, half, rope_groups):
    h = _dot(x_ref[...], w_ref[...])
    n_groups = h.shape[1] // LANES
    for gi in range(n_groups):
        hg = h[:, gi * LANES:(gi + 1) * LANES]
        if gi % rope_groups == rope_groups - 1:
            hg = _rope_lanes(hg, cos_ref[...], sin_ref[...], half)
        o_ref[:, gi * LANES:(gi + 1) * LANES] = hg.astype(o_ref.dtype)


def _mm_rope(x, w, cos, sin, bm, bn, half, rope_groups):
    m, k = x.shape
    n = w.shape[1]
    t_blocks = SEQ_LEN // bm
    vmem = 2 * (bm * k * 2 + k * bn * 2 + bm * bn * 2 + 2 * bm * LANES * 4) + 2 * bm * bn * 4 + (4 << 20)
    return pl.pallas_call(
        functools.partial(_mm_rope_kernel, half=half, rope_groups=rope_groups),
        grid=(m // bm, n // bn),
        in_specs=[pl.BlockSpec((bm, k), lambda i, j: (i, 0)),
                  pl.BlockSpec((k, bn), lambda i, j: (0, j)),
                  pl.BlockSpec((bm, LANES), lambda i, j: (i % t_blocks, 0)),
                  pl.BlockSpec((bm, LANES), lambda i, j: (i % t_blocks, 0))],
        out_specs=pl.BlockSpec((bm, bn), lambda i, j: (i, j)),
        out_shape=jax.ShapeDtypeStruct((m, n), BF16),
        compiler_params=_cparams(vmem),
    )(x, w, cos, sin)


def _mm_headnorm_rope_kernel(x_ref, w_ref, g_ref, cos_ref, sin_ref, o_ref, *, half):
    h = _dot(x_ref[...], w_ref[...])
    for gi in range(h.shape[1] // LANES):
        hg = h[:, gi * LANES:(gi + 1) * LANES]
        y = hg * lax.rsqrt(jnp.mean(hg * hg, axis=-1, keepdims=True) + NORM_EPS)
        y = y * g_ref[:, gi * LANES:(gi + 1) * LANES]
        o_ref[:, gi * LANES:(gi + 1) * LANES] = _rope_lanes(y, cos_ref[...], sin_ref[...], half).astype(o_ref.dtype)


def _mm_headnorm_rope(x, w, g, cos, sin, bm, bn, half):
    m, k = x.shape
    n = w.shape[1]
    t_blocks = SEQ_LEN // bm
    vmem = 2 * (bm * k * 2 + k * bn * 2 + bm * bn * 2 + 2 * bm * LANES * 4) + 2 * bm * bn * 4 + (4 << 20)
    return pl.pallas_call(
        functools.partial(_mm_headnorm_rope_kernel, half=half),
        grid=(m // bm, n // bn),
        in_specs=[pl.BlockSpec((bm, k), lambda i, j: (i, 0)),
                  pl.BlockSpec((k, bn), lambda i, j: (0, j)),
                  pl.BlockSpec((1, bn), lambda i, j: (0, j)),
                  pl.BlockSpec((bm, LANES), lambda i, j: (i % t_blocks, 0)),
                  pl.BlockSpec((bm, LANES), lambda i, j: (i % t_blocks, 0))],
        out_specs=pl.BlockSpec((bm, bn), lambda i, j: (i, j)),
        out_shape=jax.ShapeDtypeStruct((m, n), BF16),
        compiler_params=_cparams(vmem),
    )(x, w, g, cos, sin)


def _mm_bias_sigmoid_kernel(x_ref, w_ref, b_ref, o_ref):
    o_ref[...] = jax.nn.sigmoid(_dot(x_ref[...], w_ref[...]) + b_ref[...]).astype(o_ref.dtype)


def _mm_bias_sigmoid(x, w, b, bm, bn):
    m, k = x.shape
    n = w.shape[1]
    vmem = 2 * (bm * k * 2 + k * bn * 2 + bm * bn * 2) + 2 * bm * bn * 4 + (4 << 20)
    return pl.pallas_call(
        _mm_bias_sigmoid_kernel,
        grid=(m // bm, n // bn),
        in_specs=[pl.BlockSpec((bm, k), lambda i, j: (i, 0)),
                  pl.BlockSpec((k, bn), lambda i, j: (0, j)),
                  pl.BlockSpec((1, bn), lambda i, j: (0, j))],
        out_specs=pl.BlockSpec((bm, bn), lambda i, j: (i, j)),
        out_shape=jax.ShapeDtypeStruct((m, n), BF16),
        compiler_params=_cparams(vmem),
    )(x, w, b)


def _mm_gated_merge_kernel(oa_ref, ob_ref, wa_ref, wb_ref, ga_ref, gb_ref, o_ref):
    pa = _dot(oa_ref[...], wa_ref[...])
    pb = _dot(ob_ref[...], wb_ref[...])
    o_ref[...] = (ga_ref[...].astype(F32) * pa + gb_ref[...].astype(F32) * pb).astype(o_ref.dtype)


def _mm_gated_merge(oa, ob, wa, wb, gates, bm, bn):
    m, k = oa.shape
    n = wa.shape[1]
    nb = n // bn
    vmem = 2 * (2 * bm * k * 2 + 2 * k * bn * 2 + 3 * bm * bn * 2) + 3 * bm * bn * 4 + (4 << 20)
    return pl.pallas_call(
        _mm_gated_merge_kernel,
        grid=(m // bm, nb),
        in_specs=[pl.BlockSpec((bm, k), lambda i, j: (i, 0)),
                  pl.BlockSpec((bm, k), lambda i, j: (i, 0)),
                  pl.BlockSpec((k, bn), lambda i, j: (0, j)),
                  pl.BlockSpec((k, bn), lambda i, j: (0, j)),
                  pl.BlockSpec((bm, bn), lambda i, j: (i, j)),
                  pl.BlockSpec((bm, bn), lambda i, j: (i, j + nb))],
        out_specs=pl.BlockSpec((bm, bn), lambda i, j: (i, j)),
        out_shape=jax.ShapeDtypeStruct((m, n), BF16),
        compiler_params=_cparams(vmem),
    )(oa, ob, wa, wb, gates, gates)


def _mm_swiglu_kernel(x_ref, wg_ref, wu_ref, o_ref):
    x = x_ref[...]
    o_ref[...] = (jax.nn.silu(_dot(x, wg_ref[...])) * _dot(x, wu_ref[...])).astype(o_ref.dtype)


def _mm_swiglu(x, wg, wu, bm, bn):
    m, k = x.shape
    n_e, _, n = wg.shape
    nb = n // bn
    vmem = 2 * (bm * k * 2 + 2 * k * bn * 2 + bm * bn * 2) + 3 * bm * bn * 4 + (4 << 20)
    return pl.pallas_call(
        _mm_swiglu_kernel,
        grid=(m // bm, n_e, nb),
        in_specs=[pl.BlockSpec((bm, k), lambda i, e, j: (i, 0)),
                  pl.BlockSpec((None, k, bn), lambda i, e, j: (e, 0, j)),
                  pl.BlockSpec((None, k, bn), lambda i, e, j: (e, 0, j))],
        out_specs=pl.BlockSpec((bm, bn), lambda i, e, j: (i, e * nb + j)),
        out_shape=jax.ShapeDtypeStruct((m, n_e * n), BF16),
        compiler_params=_cparams(vmem),
    )(x, wg, wu)


def _mm_down_kernel(h_ref, w_ref, c_ref, o_ref, acc_ref, *, n_k, scaled):
    kk = pl.program_id(2)

    @pl.when(kk == 0)
    def _():
        acc_ref[...] = jnp.zeros_like(acc_ref)

    part = _dot(h_ref[...], w_ref[...])
    if scaled:
        lane = lax.broadcasted_iota(jnp.int32, c_ref.shape, 1)
        part = part * jnp.sum(jnp.where(lane == kk, c_ref[...], 0.0), axis=-1, keepdims=True)
    acc_ref[...] += part

    @pl.when(kk == n_k - 1)
    def _():
        o_ref[...] = acc_ref[...]


def _mm_down(h, w, combine, bm, bn, scaled):
    m = h.shape[0]
    n_k, bk, n = w.shape
    vmem = 2 * (bm * bk * 2 + bk * bn * 2 + bm * bn * 4 + bm * LANES * 4) + 3 * bm * bn * 4 + (4 << 20)
    return pl.pallas_call(
        functools.partial(_mm_down_kernel, n_k=n_k, scaled=scaled),
        grid=(m // bm, n // bn, n_k),
        in_specs=[pl.BlockSpec((bm, bk), lambda i, j, k: (i, k)),
                  pl.BlockSpec((None, bk, bn), lambda i, j, k: (k, 0, j)),
                  pl.BlockSpec((bm, LANES), lambda i, j, k: (i, 0))],
        out_specs=pl.BlockSpec((bm, bn), lambda i, j, k: (i, j)),
        out_shape=jax.ShapeDtypeStruct((m, n), F32),
        scratch_shapes=[pltpu.VMEM((bm, bn), F32)],
        compiler_params=_cparams(vmem),
    )(h, w, combine)


def _router_kernel(x_ref, w_ref, o_ref):
    logits = _dot(x_ref[...], w_ref[...])
    lane = lax.broadcasted_iota(jnp.int32, logits.shape, 1)
    neg = jnp.float32(-jnp.inf)
    logits = jnp.where(lane < N_EXPERTS, logits, neg)
    v1 = jnp.max(logits, axis=-1, keepdims=True)
    i1 = jnp.min(jnp.where(logits == v1, lane, LANES), axis=-1, keepdims=True)
    rest = jnp.where(lane == i1, neg, logits)
    v2 = jnp.max(rest, axis=-1, keepdims=True)
    i2 = jnp.min(jnp.where(rest == v2, lane, LANES), axis=-1, keepdims=True)
    e2 = jnp.exp(v2 - v1)
    denom = 1.0 + e2
    o_ref[...] = jnp.where(lane == i1, 1.0 / denom, 0.0) + jnp.where(lane == i2, e2 / denom, 0.0)


def _router(x, w_router_pad, bm):
    m, k = x.shape
    vmem = 2 * (bm * k * 2 + k * LANES * 2 + bm * LANES * 4) + (8 << 20)
    return pl.pallas_call(
        _router_kernel,
        grid=(m // bm,),
        in_specs=[pl.BlockSpec((bm, k), lambda i: (i, 0)),
                  pl.BlockSpec((k, LANES), lambda i: (0, 0))],
        out_specs=pl.BlockSpec((bm, LANES), lambda i: (i, 0)),
        out_shape=jax.ShapeDtypeStruct((m, LANES), F32),
        compiler_params=_cparams(vmem),
    )(x, w_router_pad)


def _ln_kernel(x_ref, f_ref, g_ref, b_ref, o_ref, ob_ref):
    z = DN_ALPHA * x_ref[...] + f_ref[...]
    mu = jnp.mean(z, axis=-1, keepdims=True)
    zc = z - mu
    var = jnp.mean(zc * zc, axis=-1, keepdims=True)
    y = zc * lax.rsqrt(var + LN_EPS) * g_ref[...] + b_ref[...]
    o_ref[...] = y
    ob_ref[...] = y.astype(BF16)


def _ln_residual(x, f, g, b, bm):
    m, d = x.shape
    vmem = 2 * (2 * bm * d * 4 + bm * d * 4 + bm * d * 2) + 3 * bm * d * 4 + (4 << 20)
    return pl.pallas_call(
        _ln_kernel,
        grid=(m // bm,),
        in_specs=[pl.BlockSpec((bm, d), lambda i: (i, 0)),
                  pl.BlockSpec((bm, d), lambda i: (i, 0)),
                  pl.BlockSpec((1, d), lambda i: (0, 0)),
                  pl.BlockSpec((1, d), lambda i: (0, 0))],
        out_specs=[pl.BlockSpec((bm, d), lambda i: (i, 0)),
                   pl.BlockSpec((bm, d), lambda i: (i, 0))],
        out_shape=[jax.ShapeDtypeStruct((m, d), F32), jax.ShapeDtypeStruct((m, d), BF16)],
        compiler_params=_cparams(vmem),
    )(x, f, g.reshape(1, d), b.reshape(1, d))


def _softmax_pv(q, k_ref, v_ref, s_ref, scale, tk):
    rows = q.shape[0]
    n_chunks = s_ref.shape[0]

    def scores(c, m):
        off = pl.multiple_of(c * tk, tk)
        s = lax.dot_general(q, k_ref[pl.ds(off, tk), :], (((1,), (1,)), ((), ())),
                            preferred_element_type=F32)
        s_ref[c] = s
        return jnp.maximum(m, jnp.max(s, axis=-1, keepdims=True))

    m = lax.fori_loop(0, n_chunks, scores, jnp.full((rows, 1), -jnp.inf, F32))

    def weighted(c, carry):
        l, acc = carry
        off = pl.multiple_of(c * tk, tk)
        p = jnp.exp((s_ref[c] - m) * scale)
        l = l + jnp.sum(p, axis=-1, keepdims=True)
        acc = acc + _dot(p.astype(BF16), v_ref[pl.ds(off, tk), :])
        return l, acc

    l, acc = lax.fori_loop(0, n_chunks, weighted,
                           (jnp.zeros((rows, 1), F32), jnp.zeros((rows, v_ref.shape[1]), F32)))
    return acc / l


def _gqa_kernel(q_ref, k_ref, v_ref, o_ref, s_ref, *, scale, tk, group):
    tq = q_ref.shape[0]
    q = jnp.concatenate([q_ref[:, h * HD_B:(h + 1) * HD_B] for h in range(group)], axis=0)
    o = _softmax_pv(q, k_ref, v_ref, s_ref, scale, tk)
    for h in range(group):
        o_ref[:, h * HD_B:(h + 1) * HD_B] = o[h * tq:(h + 1) * tq, :].astype(o_ref.dtype)


def _gqa_attention(qk, v, n_batch, tq, tk):
    n = qk.shape[0]
    group = H_B // KV_B
    qt = SEQ_LEN // tq
    vmem = 2 * (tq * group * HD_B * 2 * 2 + 2 * SEQ_LEN * HD_B * 2) + group * tq * SEQ_LEN * 4 + (12 << 20)
    return pl.pallas_call(
        functools.partial(_gqa_kernel, scale=HD_B ** -0.5, tk=tk, group=group),
        grid=(n_batch, KV_B, qt),
        in_specs=[pl.BlockSpec((tq, group * HD_B), lambda b, g, i: (b * qt + i, g)),
                  pl.BlockSpec((SEQ_LEN, HD_B), lambda b, g, i: (b, H_B + g)),
                  pl.BlockSpec((SEQ_LEN, HD_B), lambda b, g, i: (b, g))],
        out_specs=pl.BlockSpec((tq, group * HD_B), lambda b, g, i: (b * qt + i, g)),
        out_shape=jax.ShapeDtypeStruct((n, H_B * HD_B), BF16),
        scratch_shapes=[pltpu.VMEM((SEQ_LEN // tk, group * tq, tk), F32)],
        compiler_params=_cparams(vmem),
    )(qk, qk, v)


def _mla_kernel(q_ref, kn_ref, kr_ref, v_ref, o_ref, kfull_ref, s_ref, *, scale, tk):
    @pl.when(pl.program_id(2) == 0)
    def _():
        kfull_ref[:, :NOPE_A] = kn_ref[...]
        kfull_ref[:, NOPE_A:] = kr_ref[...]

    o = _softmax_pv(q_ref[...], kfull_ref, v_ref, s_ref, scale, tk)
    o_ref[...] = o.astype(o_ref.dtype)


def _mla_attention(qa, kv, kr, n_batch, tq, tk):
    n = qa.shape[0]
    qt = SEQ_LEN // tq
    vmem = (2 * (tq * HEAD_A_PAD * 2 + 3 * SEQ_LEN * LANES * 2 + tq * V_A * 2)
            + SEQ_LEN * HEAD_A_PAD * 2 + tq * SEQ_LEN * 4 + (12 << 20))
    return pl.pallas_call(
        functools.partial(_mla_kernel, scale=(NOPE_A + ROPE_A) ** -0.5, tk=tk),
        grid=(n_batch, H_A, qt),
        in_specs=[pl.BlockSpec((tq, HEAD_A_PAD), lambda b, h, i: (b * qt + i, h)),
                  pl.BlockSpec((SEQ_LEN, NOPE_A), lambda b, h, i: (b, 2 * h)),
                  pl.BlockSpec((SEQ_LEN, LANES), lambda b, h, i: (b, 0)),
                  pl.BlockSpec((SEQ_LEN, V_A), lambda b, h, i: (b, 2 * h + 1))],
        out_specs=pl.BlockSpec((tq, V_A), lambda b, h, i: (b * qt + i, h)),
        out_shape=jax.ShapeDtypeStruct((n, H_A * V_A), BF16),
        scratch_shapes=[pltpu.VMEM((SEQ_LEN, HEAD_A_PAD), BF16),
                        pltpu.VMEM((SEQ_LEN // tk, tq, tk), F32)],
        compiler_params=_cparams(vmem),
    )(qa, kv, kr, kv)


def _rope_tables(dim):
    seg = dim // 2
    inv = ROPE_THETA ** (-jnp.arange(0, seg, 2, dtype=F32) / seg)
    t = jnp.arange(SEQ_LEN)
    row = (t // GRID_W).astype(F32)
    col = (t % GRID_W).astype(F32)
    ar = row[:, None] * inv[None, :]
    ac = col[:, None] * inv[None, :]
    ar = jnp.concatenate([ar, ar], axis=-1)
    ac = jnp.concatenate([ac, ac], axis=-1)
    cos = jnp.concatenate([jnp.cos(ar), jnp.cos(ac)], axis=-1)
    sin = jnp.concatenate([jnp.sin(ar), jnp.sin(ac)], axis=-1)
    half = seg // 2
    sign = jnp.where((jnp.arange(dim) % seg) < half, -1.0, 1.0).astype(F32)
    sin = sin * sign[None, :]
    pad = LANES - dim
    if pad:
        cos = jnp.concatenate([cos, jnp.ones((SEQ_LEN, pad), F32)], axis=-1)
        sin = jnp.concatenate([sin, jnp.zeros((SEQ_LEN, pad), F32)], axis=-1)
    return cos, sin


def kernel(x_prompt, x_sample, w_in, gate_bias, q_a_norm, w_q_b, kv_a_norm, w_kv_b, q_norm, k_norm, w_br_a,
           w_br_b, w_o, ln1_g, ln1_b, ffn_w_gate, ffn_w_up, ffn_w_down, moe_w_router, moe_w_gate, moe_w_up,
           moe_w_down, ln2_g, ln2_b):
    n_prompt, n_sample = x_prompt.shape[0], x_sample.shape[0]
    n_batch = n_prompt + n_sample
    x = jnp.concatenate([x_prompt, x_sample], axis=0).reshape(n_batch * SEQ_LEN, D_MODEL)
    xb = x.astype(BF16)

    cos_a, sin_a = _rope_tables(ROPE_A)
    cos_b, sin_b = _rope_tables(HD_B)

    c0 = Q_LORA
    c1 = c0 + KV_LORA
    c2 = c1 + ROPE_A
    c3 = c2 + H_B * HD_B
    c4 = c3 + KV_B * HD_B
    c5 = c4 + KV_B * HD_B

    for l in range(DEPTH):
        wl = w_in[l]
        w_qlat = wl[:, :c0].astype(BF16)
        w_kvlat = wl[:, c0:c1].astype(BF16)
        w_krope = jnp.pad(wl[:, c1:c2], ((0, 0), (0, LANES - ROPE_A))).astype(BF16)
        w_qkb = wl[:, c2:c4].astype(BF16)
        w_vb = wl[:, c4:c5].astype(BF16)
        w_gates = wl[:, c5:].astype(BF16)
        wq = w_q_b[l].reshape(Q_LORA, H_A, NOPE_A + ROPE_A)
        wq = jnp.pad(wq, ((0, 0), (0, 0), (0, HEAD_A_PAD - NOPE_A - ROPE_A)))
        wq = wq.reshape(Q_LORA, H_A * HEAD_A_PAD).astype(BF16)
        w_kvb = w_kv_b[l].astype(BF16)
        g_qk = jnp.concatenate([jnp.tile(q_norm[l], H_B), jnp.tile(k_norm[l], KV_B)]).reshape(1, -1)
        bias = gate_bias[l].reshape(1, 2 * D_MODEL)

        q_lat = _mm_rms(xb, w_qlat, q_a_norm[l], bm=1024)
        kv_lat = _mm_rms(xb, w_kvlat, kv_a_norm[l], bm=1024)
        k_rope = _mm_rope(xb, w_krope, cos_a, sin_a, bm=1024, bn=LANES, half=ROPE_A // 4, rope_groups=1)
        qk_b = _mm_headnorm_rope(xb, w_qkb, g_qk, cos_b, sin_b, bm=1024, bn=512, half=HD_B // 4)
        v_b = _mm_plain(xb, w_vb, BF16, bm=1024, bn=512)
        gates = _mm_bias_sigmoid(xb, w_gates, bias, bm=1024, bn=1024)

        qa = _mm_rope(q_lat, wq, cos_a, sin_a, bm=1024, bn=1024, half=ROPE_A // 4, rope_groups=2)
        kv = _mm_plain(kv_lat, w_kvb, BF16, bm=1024, bn=1024)
        o_a = _mla_attention(qa, kv, k_rope, n_batch, tq=512, tk=512)
        o_b = _gqa_attention(qk_b, v_b, n_batch, tq=128, tk=512)

        y = _mm_gated_merge(o_a, o_b, w_br_a[l].astype(BF16), w_br_b[l].astype(BF16), gates, bm=1024, bn=512)
        mix = _mm_plain(y, w_o[l].astype(BF16), F32, bm=1024, bn=1024)
        x, xb = _ln_residual(x, mix, ln1_g[l], ln1_b[l], bm=256)

        i = l // 2
        if l % 2 == 0:
            pad = -D_FF_DENSE % DOWN_BK
            wg = jnp.pad(ffn_w_gate[i], ((0, 0), (0, pad))).astype(BF16)[None]
            wu = jnp.pad(ffn_w_up[i], ((0, 0), (0, pad))).astype(BF16)[None]
            wd = jnp.pad(ffn_w_down[i], ((0, pad), (0, 0))).astype(BF16).reshape(-1, DOWN_BK, D_MODEL)
            hdn = _mm_swiglu(xb, wg, wu, bm=1024, bn=512)
            f = _mm_down(hdn, wd, jnp.zeros((x.shape[0], LANES), F32), bm=1024, bn=1024, scaled=False)
        else:
            w_r = jnp.pad(moe_w_router[i], ((0, 0), (0, LANES - N_EXPERTS))).astype(BF16)
            combine = _router(xb, w_r, bm=1024)
            hdn = _mm_swiglu(xb, moe_w_gate[i].astype(BF16), moe_w_up[i].astype(BF16), bm=1024, bn=512)
            f = _mm_down(hdn, moe_w_down[i].astype(BF16), combine, bm=1024, bn=1024, scaled=True)
        x, xb = _ln_residual(x, f, ln2_g[l], ln2_b[l], bm=256)

    y = x.reshape(n_batch, SEQ_LEN, D_MODEL)
    return y[:n_prompt], y[n_prompt:]
```

```python
import functools
import math

import jax
import jax.numpy as jnp
from jax import lax
from jax.experimental import pallas as pl
from jax.experimental.pallas import tpu as pltpu

F32 = jnp.float32
BF16 = jnp.bfloat16
I32 = jnp.int32

D_MODEL = 4096
SEQ_LEN = 4096
DEPTH = 4
GRID_W = 64
ROPE_THETA = 10000.0
NORM_EPS = 1e-6
LN_EPS = 1e-5
H_A, Q_LORA, KV_LORA, NOPE_A, ROPE_A, V_A = 16, 1024, 512, 128, 64, 128
H_B, KV_B, HD_B = 16, 4, 128
D_FF_DENSE = 11008
D_FF_EXPERT = 4096
N_EXPERTS = 8
DN_ALPHA = (2 * DEPTH) ** 0.25

LANES = 128
VMEM_BUDGET = 56 * 1024 * 1024

DOWN_BK = 1024
HEAD_A_PAD = 2 * LANES
MOE_BM = 512
ROUTE_BM = 1024
DISPATCH_BM = 512
COMBINE_BM = 256
LOG2_E = math.log2(math.e)
ONES_ROWS = 16


def _cparams(vmem_bytes):
    return pltpu.CompilerParams(vmem_limit_bytes=int(min(VMEM_BUDGET, vmem_bytes)))


def _dot(a, b):
    return jnp.dot(a, b, preferred_element_type=F32)


def _rope_lanes(y, cos, sin_signed, half):
    lane = lax.broadcasted_iota(I32, y.shape, y.ndim - 1)
    lo = (lane % (2 * half)) < half
    partner = jnp.where(lo, pltpu.roll(y, LANES - half, y.ndim - 1), pltpu.roll(y, half, y.ndim - 1))
    return y * cos + partner * sin_signed


def _layernorm(z, g, b):
    mu = jnp.mean(z, axis=-1, keepdims=True)
    zc = z - mu
    var = jnp.mean(zc * zc, axis=-1, keepdims=True)
    return zc * lax.rsqrt(var + LN_EPS) * g + b


def _mm_plain_kernel(x_ref, w_ref, o_ref):
    o_ref[...] = _dot(x_ref[...], w_ref[...]).astype(o_ref.dtype)


def _mm_plain(x, w, out_dtype, bm, bn, name):
    m, k = x.shape
    n = w.shape[1]
    ob = jnp.dtype(out_dtype).itemsize
    vmem = 2 * (bm * k * 2 + k * bn * 2 + bm * bn * ob) + bm * bn * 4 + (4 << 20)
    return pl.pallas_call(
        _mm_plain_kernel,
        grid=(m // bm, n // bn),
        in_specs=[pl.BlockSpec((bm, k), lambda i, j: (i, 0)),
                  pl.BlockSpec((k, bn), lambda i, j: (0, j))],
        out_specs=pl.BlockSpec((bm, bn), lambda i, j: (i, j)),
        out_shape=jax.ShapeDtypeStruct((m, n), out_dtype),
        compiler_params=_cparams(vmem),
        name=name,
    )(x, w)


def _mm_rms_kernel(x_ref, w_ref, g_ref, o_ref):
    h = _dot(x_ref[...], w_ref[...])
    y = h * lax.rsqrt(jnp.mean(h * h, axis=-1, keepdims=True) + NORM_EPS)
    o_ref[...] = (y * g_ref[...]).astype(o_ref.dtype)


def _mm_rms(x, w, g, bm, name):
    m, k = x.shape
    n = w.shape[1]
    vmem = 2 * (bm * k * 2 + k * n * 2 + bm * n * 2) + 2 * bm * n * 4 + (4 << 20)
    return pl.pallas_call(
        _mm_rms_kernel,
        grid=(m // bm,),
        in_specs=[pl.BlockSpec((bm, k), lambda i: (i, 0)),
                  pl.BlockSpec((k, n), lambda i: (0, 0)),
                  pl.BlockSpec((1, n), lambda i: (0, 0))],
        out_specs=pl.BlockSpec((bm, n), lambda i: (i, 0)),
        out_shape=jax.ShapeDtypeStruct((m, n), BF16),
        compiler_params=_cparams(vmem),
        name=name,
    )(x, w, g.reshape(1, n).astype(F32))


def _mm_rope_kernel(x_ref, w_ref, cos_ref, sin_ref, o_ref, *, half, rope_groups):
    h = _dot(x_ref[...], w_ref[...])
    n_groups = h.shape[1] // LANES
    for gi in range(n_groups):
        hg = h[:, gi * LANES:(gi + 1) * LANES]
        if gi % rope_groups == rope_groups - 1:
            hg = _rope_lanes(hg, cos_ref[...], sin_ref[...], half)
        o_ref[:, gi * LANES:(gi + 1) * LANES] = hg.astype(o_ref.dtype)


def _mm_rope(x, w, cos, sin, bm, bn, half, rope_groups, name):
    m, k = x.shape
    n = w.shape[1]
    t_blocks = SEQ_LEN // bm
    vmem = 2 * (bm * k * 2 + k * bn * 2 + bm * bn * 2 + 2 * bm * LANES * 4) + 2 * bm * bn * 4 + (4 << 20)
    return pl.pallas_call(
        functools.partial(_mm_rope_kernel, half=half, rope_groups=rope_groups),
        grid=(m // bm, n // bn),
        in_specs=[pl.BlockSpec((bm, k), lambda i, j: (i, 0)),
                  pl.BlockSpec((k, bn), lambda i, j: (0, j)),
                  pl.BlockSpec((bm, LANES), lambda i, j: (i % t_blocks, 0)),
                  pl.BlockSpec((bm, LANES), lambda i, j: (i % t_blocks, 0))],
        out_specs=pl.BlockSpec((bm, bn), lambda i, j: (i, j)),
        out_shape=jax.ShapeDtypeStruct((m, n), BF16),
        compiler_params=_cparams(vmem),
        name=name,
    )(x, w, cos, sin)


def _mm_headnorm_rope_kernel(x_ref, w_ref, g_ref, cos_ref, sin_ref, o_ref, *, half):
    h = _dot(x_ref[...], w_ref[...])
    for gi in range(h.shape[1] // LANES):
        hg = h[:, gi * LANES:(gi + 1) * LANES]
        y = hg * lax.rsqrt(jnp.mean(hg * hg, axis=-1, keepdims=True) + NORM_EPS)
        y = y * g_ref[:, gi * LANES:(gi + 1) * LANES]
        o_ref[:, gi * LANES:(gi + 1) * LANES] = _rope_lanes(y, cos_ref[...], sin_ref[...], half).astype(o_ref.dtype)


def _mm_headnorm_rope(x, w, g, cos, sin, bm, bn, half, name):
    m, k = x.shape
    n = w.shape[1]
    t_blocks = SEQ_LEN // bm
    vmem = 2 * (bm * k * 2 + k * bn * 2 + bm * bn * 2 + 2 * bm * LANES * 4) + 2 * bm * bn * 4 + (4 << 20)
    return pl.pallas_call(
        functools.partial(_mm_headnorm_rope_kernel, half=half),
        grid=(m // bm, n // bn),
        in_specs=[pl.BlockSpec((bm, k), lambda i, j: (i, 0)),
                  pl.BlockSpec((k, bn), lambda i, j: (0, j)),
                  pl.BlockSpec((1, bn), lambda i, j: (0, j)),
                  pl.BlockSpec((bm, LANES), lambda i, j: (i % t_blocks, 0)),
                  pl.BlockSpec((bm, LANES), lambda i, j: (i % t_blocks, 0))],
        out_specs=pl.BlockSpec((bm, bn), lambda i, j: (i, j)),
        out_shape=jax.ShapeDtypeStruct((m, n), BF16),
        compiler_params=_cparams(vmem),
        name=name,
    )(x, w, g, cos, sin)


def _mm_bias_sigmoid_kernel(x_ref, w_ref, b_ref, o_ref):
    o_ref[...] = jax.nn.sigmoid(_dot(x_ref[...], w_ref[...]) + b_ref[...]).astype(o_ref.dtype)


def _mm_bias_sigmoid(x, w, b, bm, bn, name):
    m, k = x.shape
    n = w.shape[1]
    vmem = 2 * (bm * k * 2 + k * bn * 2 + bm * bn * 2) + 2 * bm * bn * 4 + (4 << 20)
    return pl.pallas_call(
        _mm_bias_sigmoid_kernel,
        grid=(m // bm, n // bn),
        in_specs=[pl.BlockSpec((bm, k), lambda i, j: (i, 0)),
                  pl.BlockSpec((k, bn), lambda i, j: (0, j)),
                  pl.BlockSpec((1, bn), lambda i, j: (0, j))],
        out_specs=pl.BlockSpec((bm, bn), lambda i, j: (i, j)),
        out_shape=jax.ShapeDtypeStruct((m, n), BF16),
        compiler_params=_cparams(vmem),
        name=name,
    )(x, w, b)


def _mm_gated_merge_kernel(oa_ref, ob_ref, wa_ref, wb_ref, ga_ref, gb_ref, o_ref):
    pa = _dot(oa_ref[...], wa_ref[...])
    pb = _dot(ob_ref[...], wb_ref[...])
    o_ref[...] = (ga_ref[...].astype(F32) * pa + gb_ref[...].astype(F32) * pb).astype(o_ref.dtype)


def _mm_gated_merge(oa, ob, wa, wb, gates, bm, bn, name):
    m, k = oa.shape
    n = wa.shape[1]
    nb = n // bn
    vmem = 2 * (2 * bm * k * 2 + 2 * k * bn * 2 + 3 * bm * bn * 2) + 3 * bm * bn * 4 + (4 << 20)
    return pl.pallas_call(
        _mm_gated_merge_kernel,
        grid=(m // bm, nb),
        in_specs=[pl.BlockSpec((bm, k), lambda i, j: (i, 0)),
                  pl.BlockSpec((bm, k), lambda i, j: (i, 0)),
                  pl.BlockSpec((k, bn), lambda i, j: (0, j)),
                  pl.BlockSpec((k, bn), lambda i, j: (0, j)),
                  pl.BlockSpec((bm, bn), lambda i, j: (i, j)),
                  pl.BlockSpec((bm, bn), lambda i, j: (i, j + nb))],
        out_specs=pl.BlockSpec((bm, bn), lambda i, j: (i, j)),
        out_shape=jax.ShapeDtypeStruct((m, n), BF16),
        compiler_params=_cparams(vmem),
        name=name,
    )(oa, ob, wa, wb, gates, gates)


def _mm_swiglu_kernel(x_ref, wg_ref, wu_ref, o_ref):
    x = x_ref[...]
    o_ref[...] = (jax.nn.silu(_dot(x, wg_ref[...])) * _dot(x, wu_ref[...])).astype(o_ref.dtype)


def _mm_swiglu(x, wg, wu, bm, bn, name):
    m, k = x.shape
    n = wg.shape[1]
    vmem = 2 * (bm * k * 2 + 2 * k * bn * 2 + bm * bn * 2) + 3 * bm * bn * 4 + (4 << 20)
    return pl.pallas_call(
        _mm_swiglu_kernel,
        grid=(m // bm, n // bn),
        in_specs=[pl.BlockSpec((bm, k), lambda i, j: (i, 0)),
                  pl.BlockSpec((k, bn), lambda i, j: (0, j)),
                  pl.BlockSpec((k, bn), lambda i, j: (0, j))],
        out_specs=pl.BlockSpec((bm, bn), lambda i, j: (i, j)),
        out_shape=jax.ShapeDtypeStruct((m, n), BF16),
        compiler_params=_cparams(vmem),
        name=name,
    )(x, wg, wu)


def _mm_down_kernel(h_ref, w_ref, o_ref, acc_ref, *, n_k):
    kk = pl.program_id(2)

    @pl.when(kk == 0)
    def _():
        acc_ref[...] = jnp.zeros_like(acc_ref)

    acc_ref[...] += _dot(h_ref[...], w_ref[...])

    @pl.when(kk == n_k - 1)
    def _():
        o_ref[...] = acc_ref[...]


def _mm_down(h, w, bm, bn, name):
    m = h.shape[0]
    n_k, bk, n = w.shape
    vmem = 2 * (bm * bk * 2 + bk * bn * 2 + bm * bn * 4) + 3 * bm * bn * 4 + (4 << 20)
    return pl.pallas_call(
        functools.partial(_mm_down_kernel, n_k=n_k),
        grid=(m // bm, n // bn, n_k),
        in_specs=[pl.BlockSpec((bm, bk), lambda i, j, k: (i, k)),
                  pl.BlockSpec((None, bk, bn), lambda i, j, k: (k, 0, j))],
        out_specs=pl.BlockSpec((bm, bn), lambda i, j, k: (i, j)),
        out_shape=jax.ShapeDtypeStruct((m, n), F32),
        scratch_shapes=[pltpu.VMEM((bm, bn), F32)],
        compiler_params=_cparams(vmem),
        name=name,
    )(h, w)


def _ln_kernel(x_ref, f_ref, g_ref, b_ref, o_ref, ob_ref):
    y = _layernorm(DN_ALPHA * x_ref[...] + f_ref[...], g_ref[...], b_ref[...])
    o_ref[...] = y
    ob_ref[...] = y.astype(BF16)


def _ln_residual(x, f, g, b, bm, name):
    m, d = x.shape
    vmem = 2 * (2 * bm * d * 4 + bm * d * 4 + bm * d * 2) + 3 * bm * d * 4 + (4 << 20)
    return pl.pallas_call(
        _ln_kernel,
        grid=(m // bm,),
        in_specs=[pl.BlockSpec((bm, d), lambda i: (i, 0)),
                  pl.BlockSpec((bm, d), lambda i: (i, 0)),
                  pl.BlockSpec((1, d), lambda i: (0, 0)),
                  pl.BlockSpec((1, d), lambda i: (0, 0))],
        out_specs=[pl.BlockSpec((bm, d), lambda i: (i, 0)),
                   pl.BlockSpec((bm, d), lambda i: (i, 0))],
        out_shape=[jax.ShapeDtypeStruct((m, d), F32), jax.ShapeDtypeStruct((m, d), BF16)],
        compiler_params=_cparams(vmem),
        name=name,
    )(x, f, g.reshape(1, d), b.reshape(1, d))


def _attention_t(q, k_ref, vt_ref, s_ref, acc_ref, exp2_scale, tk):
    n_chunks = s_ref.shape[0]
    dv = acc_ref.shape[0] - ONES_ROWS
    m = None
    for c in range(n_chunks):
        s = lax.dot_general(k_ref[c * tk:(c + 1) * tk, :], q, (((1,), (1,)), ((), ())),
                            preferred_element_type=F32)
        s_ref[c] = s
        mc = jnp.max(s, axis=0, keepdims=True)
        m = mc if m is None else jnp.maximum(m, mc)
    for c in range(n_chunks):
        p = jnp.exp2((s_ref[c] - m) * exp2_scale)
        part = _dot(vt_ref[c], p.astype(BF16))
        if c == 0:
            acc_ref[...] = part
        else:
            acc_ref[...] += part
    return acc_ref[:dv, :] / acc_ref[dv:dv + 1, :]


def _store_v_transposed(v_ref, vt_ref, tk):
    dv = v_ref.shape[1]
    for c in range(vt_ref.shape[0]):
        vt_ref[c, :dv, :] = v_ref[c * tk:(c + 1) * tk, :].astype(F32).T.astype(BF16)
        vt_ref[c, dv:, :] = jnp.ones((ONES_ROWS, tk), BF16)


def _gqa_kernel(q_ref, k_ref, v_ref, o_ref, vt_ref, s_ref, acc_ref, *, exp2_scale, tk, group):
    tq = q_ref.shape[0]

    @pl.when(pl.program_id(2) == 0)
    def _():
        _store_v_transposed(v_ref, vt_ref, tk)

    q = jnp.concatenate([q_ref[:, h * HD_B:(h + 1) * HD_B] for h in range(group)], axis=0)
    ot = _attention_t(q, k_ref, vt_ref, s_ref, acc_ref, exp2_scale, tk)
    for h in range(group):
        o_ref[:, h * HD_B:(h + 1) * HD_B] = ot[:, h * tq:(h + 1) * tq].T.astype(o_ref.dtype)


def _gqa_attention(qk, v, n_batch, tq, tk):
    n = qk.shape[0]
    group = H_B // KV_B
    qt = SEQ_LEN // tq
    m_cols = group * tq
    vmem = (2 * (2 * tq * group * HD_B * 2 + 2 * SEQ_LEN * HD_B * 2) + SEQ_LEN * HD_B * 2
            + m_cols * SEQ_LEN * 4 + HD_B * m_cols * 4 + (16 << 20))
    return pl.pallas_call(
        functools.partial(_gqa_kernel, exp2_scale=HD_B ** -0.5 * LOG2_E, tk=tk, group=group),
        grid=(n_batch, KV_B, qt),
        in_specs=[pl.BlockSpec((tq, group * HD_B), lambda b, g, i: (b * qt + i, g)),
                  pl.BlockSpec((SEQ_LEN, HD_B), lambda b, g, i: (b, H_B + g)),
                  pl.BlockSpec((SEQ_LEN, HD_B), lambda b, g, i: (b, g))],
        out_specs=pl.BlockSpec((tq, group * HD_B), lambda b, g, i: (b * qt + i, g)),
        out_shape=jax.ShapeDtypeStruct((n, H_B * HD_B), BF16),
        scratch_shapes=[pltpu.VMEM((SEQ_LEN // tk, HD_B + ONES_ROWS, tk), BF16),
                        pltpu.VMEM((SEQ_LEN // tk, tk, m_cols), F32),
                        pltpu.VMEM((HD_B + ONES_ROWS, m_cols), F32)],
        compiler_params=_cparams(vmem),
        name="gqa_attention",
    )(qk, qk, v)


def _mla_kernel(q_ref, kn_ref, kr_ref, v_ref, o_ref, kfull_ref, vt_ref, s_ref, acc_ref, *, exp2_scale, tk):
    @pl.when(pl.program_id(2) == 0)
    def _():
        kfull_ref[:, :NOPE_A] = kn_ref[...]
        kfull_ref[:, NOPE_A:] = kr_ref[...]
        _store_v_transposed(v_ref, vt_ref, tk)

    ot = _attention_t(q_ref[...], kfull_ref, vt_ref, s_ref, acc_ref, exp2_scale, tk)
    o_ref[...] = ot.T.astype(o_ref.dtype)


def _mla_attention(qa, kv, kr, n_batch, tq, tk):
    n = qa.shape[0]
    qt = SEQ_LEN // tq
    vmem = (2 * (tq * HEAD_A_PAD * 2 + 3 * SEQ_LEN * LANES * 2 + tq * V_A * 2)
            + SEQ_LEN * HEAD_A_PAD * 2 + SEQ_LEN * V_A * 2 + tq * SEQ_LEN * 4 + V_A * tq * 4 + (16 << 20))
    return pl.pallas_call(
        functools.partial(_mla_kernel, exp2_scale=(NOPE_A + ROPE_A) ** -0.5 * LOG2_E, tk=tk),
        grid=(n_batch, H_A, qt),
        in_specs=[pl.BlockSpec((tq, HEAD_A_PAD), lambda b, h, i: (b * qt + i, h)),
                  pl.BlockSpec((SEQ_LEN, NOPE_A), lambda b, h, i: (b, 2 * h)),
                  pl.BlockSpec((SEQ_LEN, LANES), lambda b, h, i: (b, 0)),
                  pl.BlockSpec((SEQ_LEN, V_A), lambda b, h, i: (b, 2 * h + 1))],
        out_specs=pl.BlockSpec((tq, V_A), lambda b, h, i: (b * qt + i, h)),
        out_shape=jax.ShapeDtypeStruct((n, H_A * V_A), BF16),
        scratch_shapes=[pltpu.VMEM((SEQ_LEN, HEAD_A_PAD), BF16),
                        pltpu.VMEM((SEQ_LEN // tk, V_A + ONES_ROWS, tk), BF16),
                        pltpu.VMEM((SEQ_LEN // tk, tk, tq), F32),
                        pltpu.VMEM((V_A + ONES_ROWS, tq), F32)],
        compiler_params=_cparams(vmem),
        name="mla_attention",
    )(qa, kv, kr, kv)


def _router_kernel(x_ref, w_ref, tri_ref, idx_ref, wgt_ref, cnt_ref, run_ref):
    @pl.when(pl.program_id(0) == 0)
    def _():
        run_ref[...] = jnp.zeros_like(run_ref)

    logits = _dot(x_ref[...], w_ref[...])
    lane = lax.broadcasted_iota(I32, logits.shape, 1)
    neg = jnp.float32(-jnp.inf)
    logits = jnp.where(lane < N_EXPERTS, logits, neg)
    v1 = jnp.max(logits, axis=-1, keepdims=True)
    i1 = jnp.min(jnp.where(logits == v1, lane, LANES), axis=-1, keepdims=True)
    rest = jnp.where(lane == i1, neg, logits)
    v2 = jnp.max(rest, axis=-1, keepdims=True)
    i2 = jnp.min(jnp.where(rest == v2, lane, LANES), axis=-1, keepdims=True)
    e2 = jnp.exp(v2 - v1)
    denom = 1.0 + e2
    wgt_ref[...] = jnp.where(lane == 0, 1.0 / denom, 0.0) + jnp.where(lane == 1, e2 / denom, 0.0)

    hit1 = lane == i1
    hit2 = lane == i2
    onehot = jnp.where(hit1 | hit2, 1.0, 0.0)
    before = _dot(tri_ref[...], onehot.astype(BF16)) + run_ref[...]
    r1 = jnp.sum(jnp.where(hit1, before, 0.0), axis=-1, keepdims=True)
    r2 = jnp.sum(jnp.where(hit2, before, 0.0), axis=-1, keepdims=True)
    idx_ref[...] = (jnp.where(lane == 0, i1, 0) + jnp.where(lane == 1, i2, 0)
                    + jnp.where(lane == 2, r1.astype(I32), 0) + jnp.where(lane == 3, r2.astype(I32), 0))
    run_ref[...] += jnp.sum(onehot, axis=0, keepdims=True)
    cnt_ref[...] = jnp.broadcast_to(run_ref[...], cnt_ref.shape)


def _router(x, w_router_pad, tri):
    m, k = x.shape
    bm = tri.shape[0]
    vmem = 2 * (bm * k * 2 + k * LANES * 2 + bm * bm * 2 + 2 * bm * LANES * 4) + (12 << 20)
    return pl.pallas_call(
        _router_kernel,
        grid=(m // bm,),
        in_specs=[pl.BlockSpec((bm, k), lambda i: (i, 0)),
                  pl.BlockSpec((k, LANES), lambda i: (0, 0)),
                  pl.BlockSpec((bm, bm), lambda i: (0, 0))],
        out_specs=[pl.BlockSpec((bm, LANES), lambda i: (i, 0)),
                   pl.BlockSpec((bm, LANES), lambda i: (i, 0)),
                   pl.BlockSpec((8, LANES), lambda i: (0, 0))],
        out_shape=[jax.ShapeDtypeStruct((m, LANES), I32),
                   jax.ShapeDtypeStruct((m, LANES), F32),
                   jax.ShapeDtypeStruct((8, LANES), F32)],
        scratch_shapes=[pltpu.VMEM((1, LANES), F32)],
        compiler_params=_cparams(vmem),
        name="moe_router",
    )(x, w_router_pad, tri)


def _row_copy(src, src_row, dst, dst_row, sem):
    return pltpu.make_async_copy(src.at[pl.ds(src_row, 1)], dst.at[pl.ds(dst_row, 1)], sem)


def _dispatch_kernel(pad_ref, pos_ref, x_hbm, xs_hbm, zero_ref, sem, zsem, *, bm):
    step = pl.program_id(0)
    base = step * bm

    def issue(r, carry):
        _row_copy(x_hbm, base + r, xs_hbm, pos_ref[0, 0, 2 * r], sem).start()
        _row_copy(x_hbm, base + r, xs_hbm, pos_ref[0, 0, 2 * r + 1], sem).start()
        return carry

    lax.fori_loop(0, bm, issue, 0)

    @pl.when(step == 0)
    def _():
        zero_ref[...] = jnp.zeros_like(zero_ref)
        for e in range(N_EXPERTS):
            lo, hi = pad_ref[2 * e], pad_ref[2 * e + 1]

            def fill(r, carry):
                _row_copy(zero_ref, 0, xs_hbm, r, zsem).start()
                return carry

            lax.fori_loop(lo, hi, fill, 0)

            def drain_fill(r, carry):
                _row_copy(zero_ref, 0, xs_hbm, r, zsem).wait()
                return carry

            lax.fori_loop(lo, hi, drain_fill, 0)

    def drain(r, carry):
        _row_copy(x_hbm, base + r, xs_hbm, pos_ref[0, 0, 2 * r], sem).wait()
        _row_copy(x_hbm, base + r, xs_hbm, pos_ref[0, 0, 2 * r + 1], sem).wait()
        return carry

    lax.fori_loop(0, bm, drain, 0)


def _dispatch(x, pos, pad_ranges, n_rows, bm):
    n, d = x.shape
    pos3 = pos.reshape(n // bm, 1, 2 * bm)
    return pl.pallas_call(
        functools.partial(_dispatch_kernel, bm=bm),
        grid_spec=pltpu.PrefetchScalarGridSpec(
            num_scalar_prefetch=1,
            grid=(n // bm,),
            in_specs=[pl.BlockSpec((1, 1, 2 * bm), lambda i, pad: (i, 0, 0), memory_space=pltpu.SMEM),
                      pl.BlockSpec(memory_space=pl.ANY)],
            out_specs=pl.BlockSpec(memory_space=pl.ANY),
            scratch_shapes=[pltpu.VMEM((8, d), F32), pltpu.SemaphoreType.DMA, pltpu.SemaphoreType.DMA],
        ),
        out_shape=jax.ShapeDtypeStruct((n_rows, d), F32),
        compiler_params=pltpu.CompilerParams(has_side_effects=True),
        name="moe_dispatch",
    )(pad_ranges, pos3, x)


def _grouped_col(i, j, used, n_col):
    return jnp.where(i < used[0], j, n_col - 1)


def _moe_swiglu_kernel(te_ref, tr_ref, used_ref, x_ref, wg_ref, wu_ref, o_ref, xb_ref):
    live = pl.program_id(0) < used_ref[0]

    @pl.when(live)
    def _():
        @pl.when(pl.program_id(1) == 0)
        def _():
            xb_ref[...] = x_ref[...].astype(BF16)

        x = xb_ref[...]
        o_ref[...] = (jax.nn.silu(_dot(x, wg_ref[...])) * _dot(x, wu_ref[...])).astype(o_ref.dtype)

    @pl.when(jnp.logical_not(live))
    def _():
        o_ref[...] = jnp.zeros_like(o_ref)


def _moe_swiglu(xs, wg, wu, tile_expert, tile_row, n_used, bn):
    p, k = xs.shape
    n = wg.shape[2]
    bm = MOE_BM
    nc = n // bn
    vmem = 2 * (bm * k * 4 + 2 * k * bn * 2 + bm * bn * 2) + bm * k * 2 + 3 * bm * bn * 4 + (4 << 20)
    return pl.pallas_call(
        _moe_swiglu_kernel,
        grid_spec=pltpu.PrefetchScalarGridSpec(
            num_scalar_prefetch=3,
            grid=(p // bm, nc),
            in_specs=[pl.BlockSpec((bm, k), lambda i, j, te, tr, u: (tr[i], 0)),
                      pl.BlockSpec((None, k, bn), lambda i, j, te, tr, u: (te[i], 0, _grouped_col(i, j, u, nc))),
                      pl.BlockSpec((None, k, bn), lambda i, j, te, tr, u: (te[i], 0, _grouped_col(i, j, u, nc)))],
            out_specs=pl.BlockSpec((bm, bn), lambda i, j, te, tr, u: (i, j)),
            scratch_shapes=[pltpu.VMEM((bm, k), BF16)],
        ),
        out_shape=jax.ShapeDtypeStruct((p, n), BF16),
        compiler_params=_cparams(vmem),
        name="moe_swiglu",
    )(tile_expert, tile_row, n_used, xs, wg, wu)


def _moe_down_kernel(te_ref, tr_ref, used_ref, h_ref, w_ref, o_ref):
    live = pl.program_id(0) < used_ref[0]

    @pl.when(live)
    def _():
        o_ref[...] = _dot(h_ref[...], w_ref[...])

    @pl.when(jnp.logical_not(live))
    def _():
        o_ref[...] = jnp.zeros_like(o_ref)


def _moe_down(h, wd, tile_expert, tile_row, n_used, bn):
    p, k = h.shape
    n = wd.shape[2]
    bm = MOE_BM
    nc = n // bn
    vmem = 2 * (bm * k * 2 + k * bn * 2 + bm * bn * 4) + bm * bn * 4 + (4 << 20)
    return pl.pallas_call(
        _moe_down_kernel,
        grid_spec=pltpu.PrefetchScalarGridSpec(
            num_scalar_prefetch=3,
            grid=(p // bm, nc),
            in_specs=[pl.BlockSpec((bm, k), lambda i, j, te, tr, u: (tr[i], 0)),
                      pl.BlockSpec((None, k, bn), lambda i, j, te, tr, u: (te[i], 0, _grouped_col(i, j, u, nc)))],
            out_specs=pl.BlockSpec((bm, bn), lambda i, j, te, tr, u: (i, j)),
        ),
        out_shape=jax.ShapeDtypeStruct((p, n), F32),
        compiler_params=_cparams(vmem),
        name="moe_down",
    )(tile_expert, tile_row, n_used, h, wd)


def _combine_ln_kernel(pos_ref, x_ref, wgt_ref, g_ref, b_ref, y_hbm, o_ref, ob_ref, ybuf_ref, sem, *, bm):
    def issue(r, carry):
        _row_copy(y_hbm, pos_ref[0, 0, 2 * r], ybuf_ref.at[0], r, sem).start()
        _row_copy(y_hbm, pos_ref[0, 0, 2 * r + 1], ybuf_ref.at[1], r, sem).start()
        return carry

    lax.fori_loop(0, bm, issue, 0)

    def drain(r, carry):
        _row_copy(y_hbm, pos_ref[0, 0, 2 * r], ybuf_ref.at[0], r, sem).wait()
        _row_copy(y_hbm, pos_ref[0, 0, 2 * r + 1], ybuf_ref.at[1], r, sem).wait()
        return carry

    lax.fori_loop(0, bm, drain, 0)
    wgt = wgt_ref[...]
    f = wgt[:, 0:1] * ybuf_ref[0] + wgt[:, 1:2] * ybuf_ref[1]
    y = _layernorm(DN_ALPHA * x_ref[...] + f, g_ref[...], b_ref[...])
    o_ref[...] = y
    ob_ref[...] = y.astype(BF16)


def _combine_ln(x, y_sorted, pos, wgt, g, b, bm):
    n, d = x.shape
    pos3 = pos.reshape(n // bm, 1, 2 * bm)
    vmem = 2 * (bm * d * 4 + bm * LANES * 4 + bm * d * 4 + bm * d * 2) + 2 * bm * d * 4 + 3 * bm * d * 4 + (4 << 20)
    return pl.pallas_call(
        functools.partial(_combine_ln_kernel, bm=bm),
        grid=(n // bm,),
        in_specs=[pl.BlockSpec((1, 1, 2 * bm), lambda i: (i, 0, 0), memory_space=pltpu.SMEM),
                  pl.BlockSpec((bm, d), lambda i: (i, 0)),
                  pl.BlockSpec((bm, LANES), lambda i: (i, 0)),
                  pl.BlockSpec((1, d), lambda i: (0, 0)),
                  pl.BlockSpec((1, d), lambda i: (0, 0)),
                  pl.BlockSpec(memory_space=pl.ANY)],
        out_specs=[pl.BlockSpec((bm, d), lambda i: (i, 0)),
                   pl.BlockSpec((bm, d), lambda i: (i, 0))],
        out_shape=[jax.ShapeDtypeStruct((n, d), F32), jax.ShapeDtypeStruct((n, d), BF16)],
        scratch_shapes=[pltpu.VMEM((2, bm, d), F32), pltpu.SemaphoreType.DMA],
        compiler_params=_cparams(vmem),
        name="moe_combine_ln",
    )(pos3, x, wgt, g.reshape(1, d), b.reshape(1, d), y_sorted)


def _moe_layer(x, xb, w_router, wg, wu, wd, ln_g, ln_b, tri):
    n = x.shape[0]
    n_tiles = (2 * n) // MOE_BM + N_EXPERTS
    idx, wgt, cnt = _router(xb, w_router, tri)
    counts = cnt[0, :N_EXPERTS].astype(I32)
    tiles_per = (counts + MOE_BM - 1) // MOE_BM
    tile_end = jnp.cumsum(tiles_per)
    offsets = (tile_end - tiles_per) * MOE_BM
    tile_id = jnp.arange(n_tiles, dtype=I32)
    last_tile = tile_end[-1] - 1
    tile_row = jnp.minimum(tile_id, last_tile)
    tile_expert = jnp.minimum(jnp.sum(tile_row[:, None] >= tile_end[None, :], axis=1), N_EXPERTS - 1).astype(I32)
    pad_hi = (offsets + tiles_per * MOE_BM).at[N_EXPERTS - 1].set(n_tiles * MOE_BM)
    pad_ranges = jnp.stack([offsets + counts, pad_hi], axis=1).reshape(-1).astype(I32)
    expert_ids = jnp.arange(N_EXPERTS, dtype=I32)
    off1 = jnp.sum(jnp.where(idx[:, 0:1] == expert_ids[None, :], offsets[None, :], 0), axis=1)
    off2 = jnp.sum(jnp.where(idx[:, 1:2] == expert_ids[None, :], offsets[None, :], 0), axis=1)
    pos = jnp.stack([off1 + idx[:, 2], off2 + idx[:, 3]], axis=1).astype(I32)

    xs = _dispatch(x, pos, pad_ranges, n_tiles * MOE_BM, DISPATCH_BM)
    n_used = tile_end[-1:].astype(I32)
    hdn = _moe_swiglu(xs, wg, wu, tile_expert, tile_row, n_used, bn=512)
    y_sorted = _moe_down(hdn, wd, tile_expert, tile_row, n_used, bn=1024)
    return _combine_ln(x, y_sorted, pos, wgt, ln_g, ln_b, COMBINE_BM)


def _rope_tables(dim):
    seg = dim // 2
    inv = ROPE_THETA ** (-jnp.arange(0, seg, 2, dtype=F32) / seg)
    t = jnp.arange(SEQ_LEN)
    row = (t // GRID_W).astype(F32)
    col = (t % GRID_W).astype(F32)
    ar = row[:, None] * inv[None, :]
    ac = col[:, None] * inv[None, :]
    ar = jnp.concatenate([ar, ar], axis=-1)
    ac = jnp.concatenate([ac, ac], axis=-1)
    cos = jnp.concatenate([jnp.cos(ar), jnp.cos(ac)], axis=-1)
    sin = jnp.concatenate([jnp.sin(ar), jnp.sin(ac)], axis=-1)
    half = seg // 2
    sign = jnp.where((jnp.arange(dim) % seg) < half, -1.0, 1.0).astype(F32)
    sin = sin * sign[None, :]
    pad = LANES - dim
    if pad:
        cos = jnp.concatenate([cos, jnp.ones((SEQ_LEN, pad), F32)], axis=-1)
        sin = jnp.concatenate([sin, jnp.zeros((SEQ_LEN, pad), F32)], axis=-1)
    return cos, sin


def kernel(x_prompt, x_sample, w_in, gate_bias, q_a_norm, w_q_b, kv_a_norm, w_kv_b, q_norm, k_norm, w_br_a,
           w_br_b, w_o, ln1_g, ln1_b, ffn_w_gate, ffn_w_up, ffn_w_down, moe_w_router, moe_w_gate, moe_w_up,
           moe_w_down, ln2_g, ln2_b):
    n_prompt, n_sample = x_prompt.shape[0], x_sample.shape[0]
    n_batch = n_prompt + n_sample
    x = jnp.concatenate([x_prompt, x_sample], axis=0).reshape(n_batch * SEQ_LEN, D_MODEL)
    xb = x.astype(BF16)

    cos_a, sin_a = _rope_tables(ROPE_A)
    cos_b, sin_b = _rope_tables(HD_B)
    tri = jnp.tril(jnp.ones((ROUTE_BM, ROUTE_BM), BF16), k=-1)

    c0 = Q_LORA
    c1 = c0 + KV_LORA
    c2 = c1 + ROPE_A
    c3 = c2 + H_B * HD_B
    c4 = c3 + KV_B * HD_B
    c5 = c4 + KV_B * HD_B

    for l in range(DEPTH):
        wl = w_in[l]
        w_qlat = wl[:, :c0].astype(BF16)
        w_kvlat = wl[:, c0:c1].astype(BF16)
        w_krope = jnp.pad(wl[:, c1:c2], ((0, 0), (0, LANES - ROPE_A))).astype(BF16)
        w_qkb = wl[:, c2:c4].astype(BF16)
        w_vb = wl[:, c4:c5].astype(BF16)
        w_gates = wl[:, c5:].astype(BF16)
        wq = w_q_b[l].reshape(Q_LORA, H_A, NOPE_A + ROPE_A)
        wq = jnp.pad(wq, ((0, 0), (0, 0), (0, HEAD_A_PAD - NOPE_A - ROPE_A)))
        wq = wq.reshape(Q_LORA, H_A * HEAD_A_PAD).astype(BF16)
        w_kvb = w_kv_b[l].astype(BF16)
        g_qk = jnp.concatenate([jnp.tile(q_norm[l], H_B), jnp.tile(k_norm[l], KV_B)]).reshape(1, -1)
        bias = gate_bias[l].reshape(1, 2 * D_MODEL)

        q_lat = _mm_rms(xb, w_qlat, q_a_norm[l], bm=1024, name="q_latent")
        kv_lat = _mm_rms(xb, w_kvlat, kv_a_norm[l], bm=1024, name="kv_latent")
        k_rope = _mm_rope(xb, w_krope, cos_a, sin_a, bm=1024, bn=LANES, half=ROPE_A // 4, rope_groups=1,
                          name="k_rope")
        qk_b = _mm_headnorm_rope(xb, w_qkb, g_qk, cos_b, sin_b, bm=1024, bn=512, half=HD_B // 4, name="qk_b")
        v_b = _mm_plain(xb, w_vb, BF16, bm=1024, bn=512, name="v_b")
        gates = _mm_bias_sigmoid(xb, w_gates, bias, bm=1024, bn=1024, name="gates")

        qa = _mm_rope(q_lat, wq, cos_a, sin_a, bm=1024, bn=1024, half=ROPE_A // 4, rope_groups=2, name="q_a")
        kv = _mm_plain(kv_lat, w_kvb, BF16, bm=1024, bn=1024, name="kv_a")
        o_a = _mla_attention(qa, kv, k_rope, n_batch, tq=512, tk=512)
        o_b = _gqa_attention(qk_b, v_b, n_batch, tq=128, tk=512)

        y = _mm_gated_merge(o_a, o_b, w_br_a[l].astype(BF16), w_br_b[l].astype(BF16), gates, bm=1024, bn=512,
                            name="gated_merge")
        mix = _mm_plain(y, w_o[l].astype(BF16), F32, bm=1024, bn=1024, name="w_o")
        x, xb = _ln_residual(x, mix, ln1_g[l], ln1_b[l], bm=256, name="ln1")

        i = l // 2
        if l % 2 == 0:
            pad = -D_FF_DENSE % DOWN_BK
            wg = jnp.pad(ffn_w_gate[i], ((0, 0), (0, pad))).astype(BF16)
            wu = jnp.pad(ffn_w_up[i], ((0, 0), (0, pad))).astype(BF16)
            wd = jnp.pad(ffn_w_down[i], ((0, pad), (0, 0))).astype(BF16).reshape(-1, DOWN_BK, D_MODEL)
            hdn = _mm_swiglu(xb, wg, wu, bm=1024, bn=512, name="ffn_swiglu")
            f = _mm_down(hdn, wd, bm=1024, bn=1024, name="ffn_down")
            x, xb = _ln_residual(x, f, ln2_g[l], ln2_b[l], bm=256, name="ln2")
        else:
            w_r = jnp.pad(moe_w_router[i], ((0, 0), (0, LANES - N_EXPERTS))).astype(BF16)
            x, xb = _moe_layer(x, xb, w_r, moe_w_gate[i].astype(BF16), moe_w_up[i].astype(BF16),
                               moe_w_down[i].astype(BF16), ln2_g[l], ln2_b[l], tri)

    y = x.reshape(n_batch, SEQ_LEN, D_MODEL)
    return y[:n_prompt], y[n_prompt:]
```

```python
import functools
import math

import jax
import jax.numpy as jnp
from jax import lax
from jax.experimental import pallas as pl
from jax.experimental.pallas import tpu as pltpu

F32 = jnp.float32
BF16 = jnp.bfloat16
I32 = jnp.int32

D_MODEL = 4096
SEQ_LEN = 4096
DEPTH = 4
GRID_W = 64
ROPE_THETA = 10000.0
NORM_EPS = 1e-6
LN_EPS = 1e-5
H_A, Q_LORA, KV_LORA, NOPE_A, ROPE_A, V_A = 16, 1024, 512, 128, 64, 128
H_B, KV_B, HD_B = 16, 4, 128
D_FF_DENSE = 11008
D_FF_EXPERT = 4096
N_EXPERTS = 8
DN_ALPHA = (2 * DEPTH) ** 0.25

LANES = 128
VMEM_BUDGET = 56 * 1024 * 1024

DOWN_BK = 1024
HEAD_A_PAD = 2 * LANES
MOE_BM = 512
ROUTE_BM = 1024
COMBINE_BM = 256
LOG2_E = math.log2(math.e)
ONES_ROWS = 16


def _cparams(vmem_bytes):
    return pltpu.CompilerParams(vmem_limit_bytes=int(min(VMEM_BUDGET, vmem_bytes)))


def _dot(a, b):
    return jnp.dot(a, b, preferred_element_type=F32)


def _rope_lanes(y, cos, sin_signed, half):
    lane = lax.broadcasted_iota(I32, y.shape, y.ndim - 1)
    lo = (lane % (2 * half)) < half
    partner = jnp.where(lo, pltpu.roll(y, LANES - half, y.ndim - 1), pltpu.roll(y, half, y.ndim - 1))
    return y * cos + partner * sin_signed


def _layernorm(z, g, b):
    mu = jnp.mean(z, axis=-1, keepdims=True)
    zc = z - mu
    var = jnp.mean(zc * zc, axis=-1, keepdims=True)
    return zc * lax.rsqrt(var + LN_EPS) * g + b


def _mm_plain_kernel(x_ref, w_ref, o_ref):
    o_ref[...] = _dot(x_ref[...], w_ref[...]).astype(o_ref.dtype)


def _mm_plain(x, w, out_dtype, bm, bn, name):
    m, k = x.shape
    n = w.shape[1]
    ob = jnp.dtype(out_dtype).itemsize
    vmem = 2 * (bm * k * 2 + k * bn * 2 + bm * bn * ob) + bm * bn * 4 + (4 << 20)
    return pl.pallas_call(
        _mm_plain_kernel,
        grid=(m // bm, n // bn),
        in_specs=[pl.BlockSpec((bm, k), lambda i, j: (i, 0)),
                  pl.BlockSpec((k, bn), lambda i, j: (0, j))],
        out_specs=pl.BlockSpec((bm, bn), lambda i, j: (i, j)),
        out_shape=jax.ShapeDtypeStruct((m, n), out_dtype),
        compiler_params=_cparams(vmem),
        name=name,
    )(x, w)


def _mm_rms_kernel(x_ref, w_ref, g_ref, o_ref):
    h = _dot(x_ref[...], w_ref[...])
    y = h * lax.rsqrt(jnp.mean(h * h, axis=-1, keepdims=True) + NORM_EPS)
    o_ref[...] = (y * g_ref[...]).astype(o_ref.dtype)


def _mm_rms(x, w, g, bm, name):
    m, k = x.shape
    n = w.shape[1]
    vmem = 2 * (bm * k * 2 + k * n * 2 + bm * n * 2) + 2 * bm * n * 4 + (4 << 20)
    return pl.pallas_call(
        _mm_rms_kernel,
        grid=(m // bm,),
        in_specs=[pl.BlockSpec((bm, k), lambda i: (i, 0)),
                  pl.BlockSpec((k, n), lambda i: (0, 0)),
                  pl.BlockSpec((1, n), lambda i: (0, 0))],
        out_specs=pl.BlockSpec((bm, n), lambda i: (i, 0)),
        out_shape=jax.ShapeDtypeStruct((m, n), BF16),
        compiler_params=_cparams(vmem),
        name=name,
    )(x, w, g.reshape(1, n).astype(F32))


def _mm_rope_kernel(x_ref, w_ref, cos_ref, sin_ref, o_ref, *, half, rope_groups):
    h = _dot(x_ref[...], w_ref[...])
    n_groups = h.shape[1] // LANES
    for gi in range(n_groups):
        hg = h[:, gi * LANES:(gi + 1) * LANES]
        if gi % rope_groups == rope_groups - 1:
            hg = _rope_lanes(hg, cos_ref[...], sin_ref[...], half)
        o_ref[:, gi * LANES:(gi + 1) * LANES] = hg.astype(o_ref.dtype)


def _mm_rope(x, w, cos, sin, bm, bn, half, rope_groups, name):
    m, k = x.shape
    n = w.shape[1]
    t_blocks = SEQ_LEN // bm
    vmem = 2 * (bm * k * 2 + k * bn * 2 + bm * bn * 2 + 2 * bm * LANES * 4) + 2 * bm * bn * 4 + (4 << 20)
    return pl.pallas_call(
        functools.partial(_mm_rope_kernel, half=half, rope_groups=rope_groups),
        grid=(m // bm, n // bn),
        in_specs=[pl.BlockSpec((bm, k), lambda i, j: (i, 0)),
                  pl.BlockSpec((k, bn), lambda i, j: (0, j)),
                  pl.BlockSpec((bm, LANES), lambda i, j: (i % t_blocks, 0)),
                  pl.BlockSpec((bm, LANES), lambda i, j: (i % t_blocks, 0))],
        out_specs=pl.BlockSpec((bm, bn), lambda i, j: (i, j)),
        out_shape=jax.ShapeDtypeStruct((m, n), BF16),
        compiler_params=_cparams(vmem),
        name=name,
    )(x, w, cos, sin)


def _mm_headnorm_rope_kernel(x_ref, w_ref, g_ref, cos_ref, sin_ref, o_ref, *, half):
    h = _dot(x_ref[...], w_ref[...])
    for gi in range(h.shape[1] // LANES):
        hg = h[:, gi * LANES:(gi + 1) * LANES]
        y = hg * lax.rsqrt(jnp.mean(hg * hg, axis=-1, keepdims=True) + NORM_EPS)
        y = y * g_ref[:, gi * LANES:(gi + 1) * LANES]
        o_ref[:, gi * LANES:(gi + 1) * LANES] = _rope_lanes(y, cos_ref[...], sin_ref[...], half).astype(o_ref.dtype)


def _mm_headnorm_rope(x, w, g, cos, sin, bm, bn, half, name):
    m, k = x.shape
    n = w.shape[1]
    t_blocks = SEQ_LEN // bm
    vmem = 2 * (bm * k * 2 + k * bn * 2 + bm * bn * 2 + 2 * bm * LANES * 4) + 2 * bm * bn * 4 + (4 << 20)
    return pl.pallas_call(
        functools.partial(_mm_headnorm_rope_kernel, half=half),
        grid=(m // bm, n // bn),
        in_specs=[pl.BlockSpec((bm, k), lambda i, j: (i, 0)),
                  pl.BlockSpec((k, bn), lambda i, j: (0, j)),
                  pl.BlockSpec((1, bn), lambda i, j: (0, j)),
                  pl.BlockSpec((bm, LANES), lambda i, j: (i % t_blocks, 0)),
                  pl.BlockSpec((bm, LANES), lambda i, j: (i % t_blocks, 0))],
        out_specs=pl.BlockSpec((bm, bn), lambda i, j: (i, j)),
        out_shape=jax.ShapeDtypeStruct((m, n), BF16),
        compiler_params=_cparams(vmem),
        name=name,
    )(x, w, g, cos, sin)


def _mm_bias_sigmoid_kernel(x_ref, w_ref, b_ref, o_ref):
    o_ref[...] = jax.nn.sigmoid(_dot(x_ref[...], w_ref[...]) + b_ref[...]).astype(o_ref.dtype)


def _mm_bias_sigmoid(x, w, b, bm, bn, name):
    m, k = x.shape
    n = w.shape[1]
    vmem = 2 * (bm * k * 2 + k * bn * 2 + bm * bn * 2) + 2 * bm * bn * 4 + (4 << 20)
    return pl.pallas_call(
        _mm_bias_sigmoid_kernel,
        grid=(m // bm, n // bn),
        in_specs=[pl.BlockSpec((bm, k), lambda i, j: (i, 0)),
                  pl.BlockSpec((k, bn), lambda i, j: (0, j)),
                  pl.BlockSpec((1, bn), lambda i, j: (0, j))],
        out_specs=pl.BlockSpec((bm, bn), lambda i, j: (i, j)),
        out_shape=jax.ShapeDtypeStruct((m, n), BF16),
        compiler_params=_cparams(vmem),
        name=name,
    )(x, w, b)


def _mm_gated_merge_kernel(oa_ref, ob_ref, wa_ref, wb_ref, ga_ref, gb_ref, o_ref):
    pa = _dot(oa_ref[...], wa_ref[...])
    pb = _dot(ob_ref[...], wb_ref[...])
    o_ref[...] = (ga_ref[...].astype(F32) * pa + gb_ref[...].astype(F32) * pb).astype(o_ref.dtype)


def _mm_gated_merge(oa, ob, wa, wb, gates, bm, bn, name):
    m, k = oa.shape
    n = wa.shape[1]
    nb = n // bn
    vmem = 2 * (2 * bm * k * 2 + 2 * k * bn * 2 + 3 * bm * bn * 2) + 3 * bm * bn * 4 + (4 << 20)
    return pl.pallas_call(
        _mm_gated_merge_kernel,
        grid=(m // bm, nb),
        in_specs=[pl.BlockSpec((bm, k), lambda i, j: (i, 0)),
                  pl.BlockSpec((bm, k), lambda i, j: (i, 0)),
                  pl.BlockSpec((k, bn), lambda i, j: (0, j)),
                  pl.BlockSpec((k, bn), lambda i, j: (0, j)),
                  pl.BlockSpec((bm, bn), lambda i, j: (i, j)),
                  pl.BlockSpec((bm, bn), lambda i, j: (i, j + nb))],
        out_specs=pl.BlockSpec((bm, bn), lambda i, j: (i, j)),
        out_shape=jax.ShapeDtypeStruct((m, n), BF16),
        compiler_params=_cparams(vmem),
        name=name,
    )(oa, ob, wa, wb, gates, gates)


def _mm_swiglu_kernel(x_ref, wg_ref, wu_ref, o_ref):
    x = x_ref[...]
    o_ref[...] = (jax.nn.silu(_dot(x, wg_ref[...])) * _dot(x, wu_ref[...])).astype(o_ref.dtype)


def _mm_swiglu(x, wg, wu, bm, bn, name):
    m, k = x.shape
    n = wg.shape[1]
    vmem = 2 * (bm * k * 2 + 2 * k * bn * 2 + bm * bn * 2) + 3 * bm * bn * 4 + (4 << 20)
    return pl.pallas_call(
        _mm_swiglu_kernel,
        grid=(m // bm, n // bn),
        in_specs=[pl.BlockSpec((bm, k), lambda i, j: (i, 0)),
                  pl.BlockSpec((k, bn), lambda i, j: (0, j)),
                  pl.BlockSpec((k, bn), lambda i, j: (0, j))],
        out_specs=pl.BlockSpec((bm, bn), lambda i, j: (i, j)),
        out_shape=jax.ShapeDtypeStruct((m, n), BF16),
        compiler_params=_cparams(vmem),
        name=name,
    )(x, wg, wu)


def _mm_down_kernel(h_ref, w_ref, o_ref, acc_ref, *, n_k):
    kk = pl.program_id(2)

    @pl.when(kk == 0)
    def _():
        acc_ref[...] = jnp.zeros_like(acc_ref)

    acc_ref[...] += _dot(h_ref[...], w_ref[...])

    @pl.when(kk == n_k - 1)
    def _():
        o_ref[...] = acc_ref[...]


def _mm_down(h, w, bm, bn, name):
    m = h.shape[0]
    n_k, bk, n = w.shape
    vmem = 2 * (bm * bk * 2 + bk * bn * 2 + bm * bn * 4) + 3 * bm * bn * 4 + (4 << 20)
    return pl.pallas_call(
        functools.partial(_mm_down_kernel, n_k=n_k),
        grid=(m // bm, n // bn, n_k),
        in_specs=[pl.BlockSpec((bm, bk), lambda i, j, k: (i, k)),
                  pl.BlockSpec((None, bk, bn), lambda i, j, k: (k, 0, j))],
        out_specs=pl.BlockSpec((bm, bn), lambda i, j, k: (i, j)),
        out_shape=jax.ShapeDtypeStruct((m, n), F32),
        scratch_shapes=[pltpu.VMEM((bm, bn), F32)],
        compiler_params=_cparams(vmem),
        name=name,
    )(h, w)


def _ln_kernel(x_ref, f_ref, g_ref, b_ref, o_ref, ob_ref):
    y = _layernorm(DN_ALPHA * x_ref[...] + f_ref[...], g_ref[...], b_ref[...])
    o_ref[...] = y
    ob_ref[...] = y.astype(BF16)


def _ln_residual(x, f, g, b, bm, name):
    m, d = x.shape
    vmem = 2 * (2 * bm * d * 4 + bm * d * 4 + bm * d * 2) + 3 * bm * d * 4 + (4 << 20)
    return pl.pallas_call(
        _ln_kernel,
        grid=(m // bm,),
        in_specs=[pl.BlockSpec((bm, d), lambda i: (i, 0)),
                  pl.BlockSpec((bm, d), lambda i: (i, 0)),
                  pl.BlockSpec((1, d), lambda i: (0, 0)),
                  pl.BlockSpec((1, d), lambda i: (0, 0))],
        out_specs=[pl.BlockSpec((bm, d), lambda i: (i, 0)),
                   pl.BlockSpec((bm, d), lambda i: (i, 0))],
        out_shape=[jax.ShapeDtypeStruct((m, d), F32), jax.ShapeDtypeStruct((m, d), BF16)],
        compiler_params=_cparams(vmem),
        name=name,
    )(x, f, g.reshape(1, d), b.reshape(1, d))


def _attention_t(q, k_ref, vt_ref, s_ref, acc_ref, exp2_scale, tk):
    n_chunks = s_ref.shape[0]
    dv = acc_ref.shape[0] - ONES_ROWS
    m = None
    for c in range(n_chunks):
        s = lax.dot_general(k_ref[c * tk:(c + 1) * tk, :], q, (((1,), (1,)), ((), ())),
                            preferred_element_type=F32)
        s_ref[c] = s
        mc = jnp.max(s, axis=0, keepdims=True)
        m = mc if m is None else jnp.maximum(m, mc)
    for c in range(n_chunks):
        p = jnp.exp2((s_ref[c] - m) * exp2_scale)
        part = _dot(vt_ref[c], p.astype(BF16))
        if c == 0:
            acc_ref[...] = part
        else:
            acc_ref[...] += part
    return acc_ref[:dv, :] / acc_ref[dv:dv + 1, :]


def _store_v_transposed(v_ref, vt_ref, tk):
    dv = v_ref.shape[1]
    for c in range(vt_ref.shape[0]):
        vt_ref[c, :dv, :] = v_ref[c * tk:(c + 1) * tk, :].astype(F32).T.astype(BF16)
        vt_ref[c, dv:, :] = jnp.ones((ONES_ROWS, tk), BF16)


def _gqa_kernel(q_ref, k_ref, v_ref, o_ref, vt_ref, s_ref, acc_ref, *, exp2_scale, tk, group):
    tq = q_ref.shape[0]

    @pl.when(pl.program_id(2) == 0)
    def _():
        _store_v_transposed(v_ref, vt_ref, tk)

    q = jnp.concatenate([q_ref[:, h * HD_B:(h + 1) * HD_B] for h in range(group)], axis=0)
    ot = _attention_t(q, k_ref, vt_ref, s_ref, acc_ref, exp2_scale, tk)
    for h in range(group):
        o_ref[:, h * HD_B:(h + 1) * HD_B] = ot[:, h * tq:(h + 1) * tq].T.astype(o_ref.dtype)


def _gqa_attention(qk, v, n_batch, tq, tk):
    n = qk.shape[0]
    group = H_B // KV_B
    qt = SEQ_LEN // tq
    m_cols = group * tq
    vmem = (2 * (2 * tq * group * HD_B * 2 + 2 * SEQ_LEN * HD_B * 2) + SEQ_LEN * HD_B * 2
            + m_cols * SEQ_LEN * 4 + HD_B * m_cols * 4 + (16 << 20))
    return pl.pallas_call(
        functools.partial(_gqa_kernel, exp2_scale=HD_B ** -0.5 * LOG2_E, tk=tk, group=group),
        grid=(n_batch, KV_B, qt),
        in_specs=[pl.BlockSpec((tq, group * HD_B), lambda b, g, i: (b * qt + i, g)),
                  pl.BlockSpec((SEQ_LEN, HD_B), lambda b, g, i: (b, H_B + g)),
                  pl.BlockSpec((SEQ_LEN, HD_B), lambda b, g, i: (b, g))],
        out_specs=pl.BlockSpec((tq, group * HD_B), lambda b, g, i: (b * qt + i, g)),
        out_shape=jax.ShapeDtypeStruct((n, H_B * HD_B), BF16),
        scratch_shapes=[pltpu.VMEM((SEQ_LEN // tk, HD_B + ONES_ROWS, tk), BF16),
                        pltpu.VMEM((SEQ_LEN // tk, tk, m_cols), F32),
                        pltpu.VMEM((HD_B + ONES_ROWS, m_cols), F32)],
        compiler_params=_cparams(vmem),
        name="gqa_attention",
    )(qk, qk, v)


def _mla_kernel(q_ref, kn_ref, kr_ref, v_ref, o_ref, kfull_ref, vt_ref, s_ref, acc_ref, *, exp2_scale, tk):
    @pl.when(pl.program_id(2) == 0)
    def _():
        kfull_ref[:, :NOPE_A] = kn_ref[...]
        kfull_ref[:, NOPE_A:] = kr_ref[...]
        _store_v_transposed(v_ref, vt_ref, tk)

    ot = _attention_t(q_ref[...], kfull_ref, vt_ref, s_ref, acc_ref, exp2_scale, tk)
    o_ref[...] = ot.T.astype(o_ref.dtype)


def _mla_attention(qa, kv, kr, n_batch, tq, tk):
    n = qa.shape[0]
    qt = SEQ_LEN // tq
    vmem = (2 * (tq * HEAD_A_PAD * 2 + 3 * SEQ_LEN * LANES * 2 + tq * V_A * 2)
            + SEQ_LEN * HEAD_A_PAD * 2 + SEQ_LEN * V_A * 2 + tq * SEQ_LEN * 4 + V_A * tq * 4 + (16 << 20))
    return pl.pallas_call(
        functools.partial(_mla_kernel, exp2_scale=(NOPE_A + ROPE_A) ** -0.5 * LOG2_E, tk=tk),
        grid=(n_batch, H_A, qt),
        in_specs=[pl.BlockSpec((tq, HEAD_A_PAD), lambda b, h, i: (b * qt + i, h)),
                  pl.BlockSpec((SEQ_LEN, NOPE_A), lambda b, h, i: (b, 2 * h)),
                  pl.BlockSpec((SEQ_LEN, LANES), lambda b, h, i: (b, 0)),
                  pl.BlockSpec((SEQ_LEN, V_A), lambda b, h, i: (b, 2 * h + 1))],
        out_specs=pl.BlockSpec((tq, V_A), lambda b, h, i: (b * qt + i, h)),
        out_shape=jax.ShapeDtypeStruct((n, H_A * V_A), BF16),
        scratch_shapes=[pltpu.VMEM((SEQ_LEN, HEAD_A_PAD), BF16),
                        pltpu.VMEM((SEQ_LEN // tk, V_A + ONES_ROWS, tk), BF16),
                        pltpu.VMEM((SEQ_LEN // tk, tk, tq), F32),
                        pltpu.VMEM((V_A + ONES_ROWS, tq), F32)],
        compiler_params=_cparams(vmem),
        name="mla_attention",
    )(qa, kv, kr, kv)


def _router_kernel(x_ref, w_ref, tri_ref, idx_ref, wgt_ref, cnt_ref, run_ref):
    @pl.when(pl.program_id(0) == 0)
    def _():
        run_ref[...] = jnp.zeros_like(run_ref)

    logits = _dot(x_ref[...], w_ref[...])
    lane = lax.broadcasted_iota(I32, logits.shape, 1)
    neg = jnp.float32(-jnp.inf)
    logits = jnp.where(lane < N_EXPERTS, logits, neg)
    v1 = jnp.max(logits, axis=-1, keepdims=True)
    i1 = jnp.min(jnp.where(logits == v1, lane, LANES), axis=-1, keepdims=True)
    rest = jnp.where(lane == i1, neg, logits)
    v2 = jnp.max(rest, axis=-1, keepdims=True)
    i2 = jnp.min(jnp.where(rest == v2, lane, LANES), axis=-1, keepdims=True)
    e2 = jnp.exp(v2 - v1)
    denom = 1.0 + e2
    wgt_ref[...] = jnp.where(lane == 0, 1.0 / denom, 0.0) + jnp.where(lane == 1, e2 / denom, 0.0)

    hit1 = lane == i1
    hit2 = lane == i2
    onehot = jnp.where(hit1 | hit2, 1.0, 0.0)
    before = _dot(tri_ref[...], onehot.astype(BF16)) + run_ref[...]
    r1 = jnp.sum(jnp.where(hit1, before, 0.0), axis=-1, keepdims=True)
    r2 = jnp.sum(jnp.where(hit2, before, 0.0), axis=-1, keepdims=True)
    idx_ref[...] = (jnp.where(lane == 0, i1, 0) + jnp.where(lane == 1, i2, 0)
                    + jnp.where(lane == 2, r1.astype(I32), 0) + jnp.where(lane == 3, r2.astype(I32), 0))
    run_ref[...] += jnp.sum(onehot, axis=0, keepdims=True)
    cnt_ref[...] = jnp.broadcast_to(run_ref[...], cnt_ref.shape)


def _router(x, w_router_pad, tri):
    m, k = x.shape
    bm = tri.shape[0]
    vmem = 2 * (bm * k * 2 + k * LANES * 2 + bm * bm * 2 + 2 * bm * LANES * 4) + (12 << 20)
    return pl.pallas_call(
        _router_kernel,
        grid=(m // bm,),
        in_specs=[pl.BlockSpec((bm, k), lambda i: (i, 0)),
                  pl.BlockSpec((k, LANES), lambda i: (0, 0)),
                  pl.BlockSpec((bm, bm), lambda i: (0, 0))],
        out_specs=[pl.BlockSpec((bm, LANES), lambda i: (i, 0)),
                   pl.BlockSpec((bm, LANES), lambda i: (i, 0)),
                   pl.BlockSpec((8, LANES), lambda i: (0, 0))],
        out_shape=[jax.ShapeDtypeStruct((m, LANES), I32),
                   jax.ShapeDtypeStruct((m, LANES), F32),
                   jax.ShapeDtypeStruct((8, LANES), F32)],
        scratch_shapes=[pltpu.VMEM((1, LANES), F32)],
        compiler_params=_cparams(vmem),
        name="moe_router",
    )(x, w_router_pad, tri)


def _row_copy(src, src_row, dst, dst_row, sem):
    return pltpu.make_async_copy(src.at[pl.ds(src_row, 1)], dst.at[pl.ds(dst_row, 1)], sem)


def _dispatch_kernel(src_ref, x_hbm, o_ref, buf_ref, sem, *, bm):
    def issue(r, carry):
        _row_copy(x_hbm, src_ref[0, 0, r], buf_ref, r, sem).start()
        return carry

    lax.fori_loop(0, bm, issue, 0)

    def drain(r, carry):
        _row_copy(x_hbm, src_ref[0, 0, r], buf_ref, r, sem).wait()
        return carry

    lax.fori_loop(0, bm, drain, 0)
    o_ref[...] = buf_ref[...].astype(o_ref.dtype)


def _dispatch(x, src, bm):
    n, d = x.shape
    p = src.shape[0]
    vmem = 2 * bm * d * 2 + 2 * bm * d * 4 + (4 << 20)
    return pl.pallas_call(
        functools.partial(_dispatch_kernel, bm=bm),
        grid=(p // bm,),
        in_specs=[pl.BlockSpec((1, 1, bm), lambda i: (i, 0, 0), memory_space=pltpu.SMEM),
                  pl.BlockSpec(memory_space=pl.ANY)],
        out_specs=pl.BlockSpec((bm, d), lambda i: (i, 0)),
        out_shape=jax.ShapeDtypeStruct((p, d), BF16),
        scratch_shapes=[pltpu.VMEM((bm, d), F32), pltpu.SemaphoreType.DMA],
        compiler_params=_cparams(vmem),
        name="moe_dispatch",
    )(src.reshape(p // bm, 1, bm), x)


def _grouped_col(i, j, used, n_col):
    return jnp.where(i < used[0], j, n_col - 1)


def _moe_swiglu_kernel(te_ref, tr_ref, used_ref, x_ref, wg_ref, wu_ref, o_ref):
    live = pl.program_id(0) < used_ref[0]

    @pl.when(live)
    def _():
        x = x_ref[...]
        o_ref[...] = (jax.nn.silu(_dot(x, wg_ref[...])) * _dot(x, wu_ref[...])).astype(o_ref.dtype)

    @pl.when(jnp.logical_not(live))
    def _():
        o_ref[...] = jnp.zeros_like(o_ref)


def _moe_swiglu(xs, wg, wu, tile_expert, tile_row, n_used, bn):
    p, k = xs.shape
    n = wg.shape[2]
    bm = MOE_BM
    nc = n // bn
    vmem = 2 * (bm * k * 2 + 2 * k * bn * 2 + bm * bn * 2) + 3 * bm * bn * 4 + (4 << 20)
    return pl.pallas_call(
        _moe_swiglu_kernel,
        grid_spec=pltpu.PrefetchScalarGridSpec(
            num_scalar_prefetch=3,
            grid=(p // bm, nc),
            in_specs=[pl.BlockSpec((bm, k), lambda i, j, te, tr, u: (tr[i], 0)),
                      pl.BlockSpec((None, k, bn), lambda i, j, te, tr, u: (te[i], 0, _grouped_col(i, j, u, nc))),
                      pl.BlockSpec((None, k, bn), lambda i, j, te, tr, u: (te[i], 0, _grouped_col(i, j, u, nc)))],
            out_specs=pl.BlockSpec((bm, bn), lambda i, j, te, tr, u: (i, j)),
        ),
        out_shape=jax.ShapeDtypeStruct((p, n), BF16),
        compiler_params=_cparams(vmem),
        name="moe_swiglu",
    )(tile_expert, tile_row, n_used, xs, wg, wu)


def _moe_down_kernel(te_ref, tr_ref, used_ref, h_ref, w_ref, o_ref):
    live = pl.program_id(0) < used_ref[0]

    @pl.when(live)
    def _():
        o_ref[...] = _dot(h_ref[...], w_ref[...])

    @pl.when(jnp.logical_not(live))
    def _():
        o_ref[...] = jnp.zeros_like(o_ref)


def _moe_down(h, wd, tile_expert, tile_row, n_used, bn):
    p, k = h.shape
    n = wd.shape[2]
    bm = MOE_BM
    nc = n // bn
    vmem = 2 * (bm * k * 2 + k * bn * 2 + bm * bn * 4) + bm * bn * 4 + (4 << 20)
    return pl.pallas_call(
        _moe_down_kernel,
        grid_spec=pltpu.PrefetchScalarGridSpec(
            num_scalar_prefetch=3,
            grid=(p // bm, nc),
            in_specs=[pl.BlockSpec((bm, k), lambda i, j, te, tr, u: (tr[i], 0)),
                      pl.BlockSpec((None, k, bn), lambda i, j, te, tr, u: (te[i], 0, _grouped_col(i, j, u, nc)))],
            out_specs=pl.BlockSpec((bm, bn), lambda i, j, te, tr, u: (i, j)),
        ),
        out_shape=jax.ShapeDtypeStruct((p, n), F32),
        compiler_params=_cparams(vmem),
        name="moe_down",
    )(tile_expert, tile_row, n_used, h, wd)


def _combine_ln_kernel(pos_ref, x_ref, wgt_ref, g_ref, b_ref, y_hbm, o_ref, ob_ref, ybuf_ref, sem, *, bm):
    def issue(r, carry):
        _row_copy(y_hbm, pos_ref[0, 0, 2 * r], ybuf_ref.at[0], r, sem).start()
        _row_copy(y_hbm, pos_ref[0, 0, 2 * r + 1], ybuf_ref.at[1], r, sem).start()
        return carry

    lax.fori_loop(0, bm, issue, 0)

    def drain(r, carry):
        _row_copy(y_hbm, pos_ref[0, 0, 2 * r], ybuf_ref.at[0], r, sem).wait()
        _row_copy(y_hbm, pos_ref[0, 0, 2 * r + 1], ybuf_ref.at[1], r, sem).wait()
        return carry

    lax.fori_loop(0, bm, drain, 0)
    wgt = wgt_ref[...]
    f = wgt[:, 0:1] * ybuf_ref[0] + wgt[:, 1:2] * ybuf_ref[1]
    y = _layernorm(DN_ALPHA * x_ref[...] + f, g_ref[...], b_ref[...])
    o_ref[...] = y
    ob_ref[...] = y.astype(BF16)


def _combine_ln(x, y_sorted, pos, wgt, g, b, bm):
    n, d = x.shape
    pos3 = pos.reshape(n // bm, 1, 2 * bm)
    vmem = 2 * (bm * d * 4 + bm * LANES * 4 + bm * d * 4 + bm * d * 2) + 2 * bm * d * 4 + 3 * bm * d * 4 + (4 << 20)
    return pl.pallas_call(
        functools.partial(_combine_ln_kernel, bm=bm),
        grid=(n // bm,),
        in_specs=[pl.BlockSpec((1, 1, 2 * bm), lambda i: (i, 0, 0), memory_space=pltpu.SMEM),
                  pl.BlockSpec((bm, d), lambda i: (i, 0)),
                  pl.BlockSpec((bm, LANES), lambda i: (i, 0)),
                  pl.BlockSpec((1, d), lambda i: (0, 0)),
                  pl.BlockSpec((1, d), lambda i: (0, 0)),
                  pl.BlockSpec(memory_space=pl.ANY)],
        out_specs=[pl.BlockSpec((bm, d), lambda i: (i, 0)),
                   pl.BlockSpec((bm, d), lambda i: (i, 0))],
        out_shape=[jax.ShapeDtypeStruct((n, d), F32), jax.ShapeDtypeStruct((n, d), BF16)],
        scratch_shapes=[pltpu.VMEM((2, bm, d), F32), pltpu.SemaphoreType.DMA],
        compiler_params=_cparams(vmem),
        name="moe_combine_ln",
    )(pos3, x, wgt, g.reshape(1, d), b.reshape(1, d), y_sorted)


def _moe_layer(x, xb, w_router, wg, wu, wd, ln_g, ln_b, tri):
    n = x.shape[0]
    n_tiles = (2 * n) // MOE_BM + N_EXPERTS
    idx, wgt, cnt = _router(xb, w_router, tri)
    counts = cnt[0, :N_EXPERTS].astype(I32)
    tiles_per = (counts + MOE_BM - 1) // MOE_BM
    tile_end = jnp.cumsum(tiles_per)
    offsets = (tile_end - tiles_per) * MOE_BM
    tile_id = jnp.arange(n_tiles, dtype=I32)
    last_tile = tile_end[-1] - 1
    tile_row = jnp.minimum(tile_id, last_tile)
    tile_expert = jnp.minimum(jnp.sum(tile_row[:, None] >= tile_end[None, :], axis=1), N_EXPERTS - 1).astype(I32)
    order = jnp.argsort(idx[:, 0:2].reshape(-1), stable=True).astype(I32)
    group_start = jnp.cumsum(counts) - counts
    rank = jnp.arange(MOE_BM, dtype=I32)[None, :] + ((tile_row - (tile_end - tiles_per)[tile_expert]) * MOE_BM)[:, None]
    valid = (rank < counts[tile_expert][:, None]) & (tile_id <= last_tile)[:, None]
    slot = jnp.clip(group_start[tile_expert][:, None] + rank, 0, 2 * n - 1)
    src = jnp.where(valid, order[slot] // 2, 0).reshape(-1).astype(I32)
    expert_ids = jnp.arange(N_EXPERTS, dtype=I32)
    off1 = jnp.sum(jnp.where(idx[:, 0:1] == expert_ids[None, :], offsets[None, :], 0), axis=1)
    off2 = jnp.sum(jnp.where(idx[:, 1:2] == expert_ids[None, :], offsets[None, :], 0), axis=1)
    pos = jnp.stack([off1 + idx[:, 2], off2 + idx[:, 3]], axis=1).astype(I32)

    xs = _dispatch(x, src, MOE_BM)
    n_used = tile_end[-1:].astype(I32)
    hdn = _moe_swiglu(xs, wg, wu, tile_expert, tile_row, n_used, bn=512)
    y_sorted = _moe_down(hdn, wd, tile_expert, tile_row, n_used, bn=1024)
    return _combine_ln(x, y_sorted, pos, wgt, ln_g, ln_b, COMBINE_BM)


def _rope_tables(dim):
    seg = dim // 2
    inv = ROPE_THETA ** (-jnp.arange(0, seg, 2, dtype=F32) / seg)
    t = jnp.arange(SEQ_LEN)
    row = (t // GRID_W).astype(F32)
    col = (t % GRID_W).astype(F32)
    ar = row[:, None] * inv[None, :]
    ac = col[:, None] * inv[None, :]
    ar = jnp.concatenate([ar, ar], axis=-1)
    ac = jnp.concatenate([ac, ac], axis=-1)
    cos = jnp.concatenate([jnp.cos(ar), jnp.cos(ac)], axis=-1)
    sin = jnp.concatenate([jnp.sin(ar), jnp.sin(ac)], axis=-1)
    half = seg // 2
    sign = jnp.where((jnp.arange(dim) % seg) < half, -1.0, 1.0).astype(F32)
    sin = sin * sign[None, :]
    pad = LANES - dim
    if pad:
        cos = jnp.concatenate([cos, jnp.ones((SEQ_LEN, pad), F32)], axis=-1)
        sin = jnp.concatenate([sin, jnp.zeros((SEQ_LEN, pad), F32)], axis=-1)
    return cos, sin


def kernel(x_prompt, x_sample, w_in, gate_bias, q_a_norm, w_q_b, kv_a_norm, w_kv_b, q_norm, k_norm, w_br_a,
           w_br_b, w_o, ln1_g, ln1_b, ffn_w_gate, ffn_w_up, ffn_w_down, moe_w_router, moe_w_gate, moe_w_up,
           moe_w_down, ln2_g, ln2_b):
    n_prompt, n_sample = x_prompt.shape[0], x_sample.shape[0]
    n_batch = n_prompt + n_sample
    x = jnp.concatenate([x_prompt, x_sample], axis=0).reshape(n_batch * SEQ_LEN, D_MODEL)
    xb = x.astype(BF16)

    cos_a, sin_a = _rope_tables(ROPE_A)
    cos_b, sin_b = _rope_tables(HD_B)
    tri = jnp.tril(jnp.ones((ROUTE_BM, ROUTE_BM), BF16), k=-1)

    c0 = Q_LORA
    c1 = c0 + KV_LORA
    c2 = c1 + ROPE_A
    c3 = c2 + H_B * HD_B
    c4 = c3 + KV_B * HD_B
    c5 = c4 + KV_B * HD_B

    for l in range(DEPTH):
        wl = w_in[l]
        w_qlat = wl[:, :c0].astype(BF16)
        w_kvlat = wl[:, c0:c1].astype(BF16)
        w_krope = jnp.pad(wl[:, c1:c2], ((0, 0), (0, LANES - ROPE_A))).astype(BF16)
        w_qkb = wl[:, c2:c4].astype(BF16)
        w_vb = wl[:, c4:c5].astype(BF16)
        w_gates = wl[:, c5:].astype(BF16)
        wq = w_q_b[l].reshape(Q_LORA, H_A, NOPE_A + ROPE_A)
        wq = jnp.pad(wq, ((0, 0), (0, 0), (0, HEAD_A_PAD - NOPE_A - ROPE_A)))
        wq = wq.reshape(Q_LORA, H_A * HEAD_A_PAD).astype(BF16)
        w_kvb = w_kv_b[l].astype(BF16)
        g_qk = jnp.concatenate([jnp.tile(q_norm[l], H_B), jnp.tile(k_norm[l], KV_B)]).reshape(1, -1)
        bias = gate_bias[l].reshape(1, 2 * D_MODEL)

        q_lat = _mm_rms(xb, w_qlat, q_a_norm[l], bm=1024, name="q_latent")
        kv_lat = _mm_rms(xb, w_kvlat, kv_a_norm[l], bm=1024, name="kv_latent")
        k_rope = _mm_rope(xb, w_krope, cos_a, sin_a, bm=1024, bn=LANES, half=ROPE_A // 4, rope_groups=1,
                          name="k_rope")
        qk_b = _mm_headnorm_rope(xb, w_qkb, g_qk, cos_b, sin_b, bm=1024, bn=512, half=HD_B // 4, name="qk_b")
        v_b = _mm_plain(xb, w_vb, BF16, bm=1024, bn=512, name="v_b")
        gates = _mm_bias_sigmoid(xb, w_gates, bias, bm=1024, bn=1024, name="gates")

        qa = _mm_rope(q_lat, wq, cos_a, sin_a, bm=1024, bn=1024, half=ROPE_A // 4, rope_groups=2, name="q_a")
        kv = _mm_plain(kv_lat, w_kvb, BF16, bm=1024, bn=1024, name="kv_a")
        o_a = _mla_attention(qa, kv, k_rope, n_batch, tq=512, tk=512)
        o_b = _gqa_attention(qk_b, v_b, n_batch, tq=128, tk=512)

        y = _mm_gated_merge(o_a, o_b, w_br_a[l].astype(BF16), w_br_b[l].astype(BF16), gates, bm=1024, bn=512,
                            name="gated_merge")
        mix = _mm_plain(y, w_o[l].astype(BF16), F32, bm=1024, bn=1024, name="w_o")
        x, xb = _ln_residual(x, mix, ln1_g[l], ln1_b[l], bm=256, name="ln1")

        i = l // 2
        if l % 2 == 0:
            pad = -D_FF_DENSE % DOWN_BK
            wg = jnp.pad(ffn_w_gate[i], ((0, 0), (0, pad))).astype(BF16)
            wu = jnp.pad(ffn_w_up[i], ((0, 0), (0, pad))).astype(BF16)
            wd = jnp.pad(ffn_w_down[i], ((0, pad), (0, 0))).astype(BF16).reshape(-1, DOWN_BK, D_MODEL)
            hdn = _mm_swiglu(xb, wg, wu, bm=1024, bn=512, name="ffn_swiglu")
            f = _mm_down(hdn, wd, bm=1024, bn=1024, name="ffn_down")
            x, xb = _ln_residual(x, f, ln2_g[l], ln2_b[l], bm=256, name="ln2")
        else:
            w_r = jnp.pad(moe_w_router[i], ((0, 0), (0, LANES - N_EXPERTS))).astype(BF16)
            x, xb = _moe_layer(x, xb, w_r, moe_w_gate[i].astype(BF16), moe_w_up[i].astype(BF16),
                               moe_w_down[i].astype(BF16), ln2_g[l], ln2_b[l], tri)

    y = x.reshape(n_batch, SEQ_LEN, D_MODEL)
    return y[:n_prompt], y[n_prompt:]
```

```python
import functools
import math

import jax
import jax.numpy as jnp
from jax import lax
from jax.experimental import pallas as pl
from jax.experimental.pallas import tpu as pltpu

F32 = jnp.float32
BF16 = jnp.bfloat16
I32 = jnp.int32

D_MODEL = 4096
SEQ_LEN = 4096
DEPTH = 4
GRID_W = 64
ROPE_THETA = 10000.0
NORM_EPS = 1e-6
LN_EPS = 1e-5
H_A, Q_LORA, KV_LORA, NOPE_A, ROPE_A, V_A = 16, 1024, 512, 128, 64, 128
H_B, KV_B, HD_B = 16, 4, 128
D_FF_DENSE = 11008
D_FF_EXPERT = 4096
N_EXPERTS = 8
DN_ALPHA = (2 * DEPTH) ** 0.25

LANES = 128
VMEM_BUDGET = 56 * 1024 * 1024

DOWN_BK = 1024
HEAD_A_PAD = 2 * LANES
MOE_BM = 512
ROUTE_BM = 1024
COMBINE_BM = 256
LOG2_E = math.log2(math.e)
ONES_ROWS = 16
ATTN_UNROLL = 4
EPILOGUE_ROWS = 256


def _cparams(vmem_bytes):
    return pltpu.CompilerParams(vmem_limit_bytes=int(min(VMEM_BUDGET, vmem_bytes)))


def _dot(a, b):
    return jnp.dot(a, b, preferred_element_type=F32)


def _rope_lanes(y, cos, sin_signed, half):
    lane = lax.broadcasted_iota(I32, y.shape, y.ndim - 1)
    lo = (lane % (2 * half)) < half
    partner = jnp.where(lo, pltpu.roll(y, LANES - half, y.ndim - 1), pltpu.roll(y, half, y.ndim - 1))
    return y * cos + partner * sin_signed


def _layernorm(z, g, b):
    mu = jnp.mean(z, axis=-1, keepdims=True)
    zc = z - mu
    var = jnp.mean(zc * zc, axis=-1, keepdims=True)
    return zc * lax.rsqrt(var + LN_EPS) * g + b


def _mm_plain_kernel(x_ref, w_ref, o_ref):
    o_ref[...] = _dot(x_ref[...], w_ref[...]).astype(o_ref.dtype)


def _mm_plain(x, w, out_dtype, bm, bn, name, n=None, col0=0):
    m, k = x.shape
    n = w.shape[1] if n is None else n
    ob = jnp.dtype(out_dtype).itemsize
    vmem = 2 * (bm * k * 2 + k * bn * 2 + bm * bn * ob) + bm * bn * 4 + (4 << 20)
    return pl.pallas_call(
        _mm_plain_kernel,
        grid=(m // bm, n // bn),
        in_specs=[pl.BlockSpec((bm, k), lambda i, j: (i, 0)),
                  pl.BlockSpec((k, bn), lambda i, j: (0, j + col0))],
        out_specs=pl.BlockSpec((bm, bn), lambda i, j: (i, j)),
        out_shape=jax.ShapeDtypeStruct((m, n), out_dtype),
        compiler_params=_cparams(vmem),
        name=name,
    )(x, w)


def _mm_rms_kernel(x_ref, w_ref, g_ref, o_ref):
    h = _dot(x_ref[...], w_ref[...])
    y = h * lax.rsqrt(jnp.mean(h * h, axis=-1, keepdims=True) + NORM_EPS)
    o_ref[...] = (y * g_ref[...]).astype(o_ref.dtype)


def _mm_rms(x, w, g, bm, name, col0=0):
    m, k = x.shape
    n = g.shape[0]
    vmem = 2 * (bm * k * 2 + k * n * 2 + bm * n * 2) + 2 * bm * n * 4 + (4 << 20)
    return pl.pallas_call(
        _mm_rms_kernel,
        grid=(m // bm,),
        in_specs=[pl.BlockSpec((bm, k), lambda i: (i, 0)),
                  pl.BlockSpec((k, n), lambda i: (0, col0)),
                  pl.BlockSpec((1, n), lambda i: (0, 0))],
        out_specs=pl.BlockSpec((bm, n), lambda i: (i, 0)),
        out_shape=jax.ShapeDtypeStruct((m, n), BF16),
        compiler_params=_cparams(vmem),
        name=name,
    )(x, w, g.reshape(1, n).astype(F32))


def _mm_rope_kernel(x_ref, w_ref, cos_ref, sin_ref, o_ref, *, half, rope_groups):
    h = _dot(x_ref[...], w_ref[...])
    n_groups = h.shape[1] // LANES
    for gi in range(n_groups):
        hg = h[:, gi * LANES:(gi + 1) * LANES]
        if gi % rope_groups == rope_groups - 1:
            hg = _rope_lanes(hg, cos_ref[...], sin_ref[...], half)
        o_ref[:, gi * LANES:(gi + 1) * LANES] = hg.astype(o_ref.dtype)


def _mm_rope(x, w, cos, sin, bm, bn, half, rope_groups, name, n=None, col0=0):
    m, k = x.shape
    n = w.shape[1] if n is None else n
    t_blocks = SEQ_LEN // bm
    vmem = 2 * (bm * k * 2 + k * bn * 2 + bm * bn * 2 + 2 * bm * LANES * 4) + 2 * bm * bn * 4 + (4 << 20)
    return pl.pallas_call(
        functools.partial(_mm_rope_kernel, half=half, rope_groups=rope_groups),
        grid=(m // bm, n // bn),
        in_specs=[pl.BlockSpec((bm, k), lambda i, j: (i, 0)),
                  pl.BlockSpec((k, bn), lambda i, j: (0, j + col0)),
                  pl.BlockSpec((bm, LANES), lambda i, j: (i % t_blocks, 0)),
                  pl.BlockSpec((bm, LANES), lambda i, j: (i % t_blocks, 0))],
        out_specs=pl.BlockSpec((bm, bn), lambda i, j: (i, j)),
        out_shape=jax.ShapeDtypeStruct((m, n), BF16),
        compiler_params=_cparams(vmem),
        name=name,
    )(x, w, cos, sin)


def _mm_headnorm_rope_kernel(x_ref, w_ref, g_ref, cos_ref, sin_ref, o_ref, *, half):
    for r0 in range(0, x_ref.shape[0], EPILOGUE_ROWS):
        rows = slice(r0, r0 + EPILOGUE_ROWS)
        h = _dot(x_ref[rows, :], w_ref[...])
        for gi in range(h.shape[1] // LANES):
            cols = slice(gi * LANES, (gi + 1) * LANES)
            hg = h[:, cols]
            y = hg * lax.rsqrt(jnp.mean(hg * hg, axis=-1, keepdims=True) + NORM_EPS)
            y = y * g_ref[:, cols]
            o_ref[rows, cols] = _rope_lanes(y, cos_ref[rows, :], sin_ref[rows, :], half).astype(o_ref.dtype)


def _mm_headnorm_rope(x, w, g, cos, sin, bm, bn, half, name):
    m, k = x.shape
    n = g.shape[1]
    t_blocks = SEQ_LEN // bm
    vmem = 2 * (bm * k * 2 + k * bn * 2 + bm * bn * 2 + 2 * bm * LANES * 4) + 2 * bm * bn * 4 + (4 << 20)
    return pl.pallas_call(
        functools.partial(_mm_headnorm_rope_kernel, half=half),
        grid=(m // bm, n // bn),
        in_specs=[pl.BlockSpec((bm, k), lambda i, j: (i, 0)),
                  pl.BlockSpec((k, bn), lambda i, j: (0, j)),
                  pl.BlockSpec((1, bn), lambda i, j: (0, j)),
                  pl.BlockSpec((bm, LANES), lambda i, j: (i % t_blocks, 0)),
                  pl.BlockSpec((bm, LANES), lambda i, j: (i % t_blocks, 0))],
        out_specs=pl.BlockSpec((bm, bn), lambda i, j: (i, j)),
        out_shape=jax.ShapeDtypeStruct((m, n), BF16),
        compiler_params=_cparams(vmem),
        name=name,
    )(x, w, g, cos, sin)


def _mm_bias_sigmoid_kernel(x_ref, w_ref, b_ref, o_ref):
    o_ref[...] = jax.nn.sigmoid(_dot(x_ref[...], w_ref[...]) + b_ref[...]).astype(o_ref.dtype)


def _mm_bias_sigmoid(x, w, b, bm, bn, name, col0=0):
    m, k = x.shape
    n = b.shape[1]
    vmem = 2 * (bm * k * 2 + k * bn * 2 + bm * bn * 2) + 2 * bm * bn * 4 + (4 << 20)
    return pl.pallas_call(
        _mm_bias_sigmoid_kernel,
        grid=(m // bm, n // bn),
        in_specs=[pl.BlockSpec((bm, k), lambda i, j: (i, 0)),
                  pl.BlockSpec((k, bn), lambda i, j: (0, j + col0)),
                  pl.BlockSpec((1, bn), lambda i, j: (0, j))],
        out_specs=pl.BlockSpec((bm, bn), lambda i, j: (i, j)),
        out_shape=jax.ShapeDtypeStruct((m, n), BF16),
        compiler_params=_cparams(vmem),
        name=name,
    )(x, w, b)


def _mm_gated_merge_kernel(oa_ref, ob_ref, wa_ref, wb_ref, ga_ref, gb_ref, o_ref):
    pa = _dot(oa_ref[...], wa_ref[...])
    pb = _dot(ob_ref[...], wb_ref[...])
    o_ref[...] = (ga_ref[...].astype(F32) * pa + gb_ref[...].astype(F32) * pb).astype(o_ref.dtype)


def _mm_gated_merge(oa, ob, wa, wb, gates, bm, bn, name):
    m, k = oa.shape
    n = wa.shape[1]
    nb = n // bn
    vmem = 2 * (2 * bm * k * 2 + 2 * k * bn * 2 + 3 * bm * bn * 2) + 3 * bm * bn * 4 + (4 << 20)
    return pl.pallas_call(
        _mm_gated_merge_kernel,
        grid=(m // bm, nb),
        in_specs=[pl.BlockSpec((bm, k), lambda i, j: (i, 0)),
                  pl.BlockSpec((bm, k), lambda i, j: (i, 0)),
                  pl.BlockSpec((k, bn), lambda i, j: (0, j)),
                  pl.BlockSpec((k, bn), lambda i, j: (0, j)),
                  pl.BlockSpec((bm, bn), lambda i, j: (i, j)),
                  pl.BlockSpec((bm, bn), lambda i, j: (i, j + nb))],
        out_specs=pl.BlockSpec((bm, bn), lambda i, j: (i, j)),
        out_shape=jax.ShapeDtypeStruct((m, n), BF16),
        compiler_params=_cparams(vmem),
        name=name,
    )(oa, ob, wa, wb, gates, gates)


def _mm_swiglu_kernel(x_ref, wg_ref, wu_ref, o_ref):
    x = x_ref[...]
    o_ref[...] = (jax.nn.silu(_dot(x, wg_ref[...])) * _dot(x, wu_ref[...])).astype(o_ref.dtype)


def _mm_swiglu(x, wg, wu, bm, bn, name):
    m, k = x.shape
    n = wg.shape[1]
    vmem = 2 * (bm * k * 2 + 2 * k * bn * 2 + bm * bn * 2) + 3 * bm * bn * 4 + (4 << 20)
    return pl.pallas_call(
        _mm_swiglu_kernel,
        grid=(m // bm, n // bn),
        in_specs=[pl.BlockSpec((bm, k), lambda i, j: (i, 0)),
                  pl.BlockSpec((k, bn), lambda i, j: (0, j)),
                  pl.BlockSpec((k, bn), lambda i, j: (0, j))],
        out_specs=pl.BlockSpec((bm, bn), lambda i, j: (i, j)),
        out_shape=jax.ShapeDtypeStruct((m, n), BF16),
        compiler_params=_cparams(vmem),
        name=name,
    )(x, wg, wu)


def _mm_down_kernel(h_ref, w_ref, o_ref, acc_ref, *, n_k):
    kk = pl.program_id(2)

    @pl.when(kk == 0)
    def _():
        acc_ref[...] = jnp.zeros_like(acc_ref)

    acc_ref[...] += _dot(h_ref[...], w_ref[...])

    @pl.when(kk == n_k - 1)
    def _():
        o_ref[...] = acc_ref[...]


def _mm_down(h, w, bm, bn, name):
    m = h.shape[0]
    n_k, bk, n = w.shape
    vmem = 2 * (bm * bk * 2 + bk * bn * 2 + bm * bn * 4) + 3 * bm * bn * 4 + (4 << 20)
    return pl.pallas_call(
        functools.partial(_mm_down_kernel, n_k=n_k),
        grid=(m // bm, n // bn, n_k),
        in_specs=[pl.BlockSpec((bm, bk), lambda i, j, k: (i, k)),
                  pl.BlockSpec((None, bk, bn), lambda i, j, k: (k, 0, j))],
        out_specs=pl.BlockSpec((bm, bn), lambda i, j, k: (i, j)),
        out_shape=jax.ShapeDtypeStruct((m, n), F32),
        scratch_shapes=[pltpu.VMEM((bm, bn), F32)],
        compiler_params=_cparams(vmem),
        name=name,
    )(h, w)


def _ln_kernel(x_ref, f_ref, g_ref, b_ref, o_ref, ob_ref):
    y = _layernorm(DN_ALPHA * x_ref[...] + f_ref[...], g_ref[...], b_ref[...])
    o_ref[...] = y
    ob_ref[...] = y.astype(BF16)


def _ln_residual(x, f, g, b, bm, name):
    m, d = x.shape
    vmem = 2 * (2 * bm * d * 4 + bm * d * 4 + bm * d * 2) + 3 * bm * d * 4 + (4 << 20)
    return pl.pallas_call(
        _ln_kernel,
        grid=(m // bm,),
        in_specs=[pl.BlockSpec((bm, d), lambda i: (i, 0)),
                  pl.BlockSpec((bm, d), lambda i: (i, 0)),
                  pl.BlockSpec((1, d), lambda i: (0, 0)),
                  pl.BlockSpec((1, d), lambda i: (0, 0))],
        out_specs=[pl.BlockSpec((bm, d), lambda i: (i, 0)),
                   pl.BlockSpec((bm, d), lambda i: (i, 0))],
        out_shape=[jax.ShapeDtypeStruct((m, d), F32), jax.ShapeDtypeStruct((m, d), BF16)],
        compiler_params=_cparams(vmem),
        name=name,
    )(x, f, g.reshape(1, d), b.reshape(1, d))


def _attention_step(q, k_ref, vt_ref, s_cur, s_prev, m_cur, m_prev, acc_ref, exp2_scale, tk):
    n_chunks = s_cur.shape[0]
    assert n_chunks % ATTN_UNROLL == 0
    dv = acc_ref.shape[0] - ONES_ROWS
    m_old = m_prev[...]
    acc_ref[...] = jnp.zeros_like(acc_ref)

    def body(gi, m):
        for u in range(ATTN_UNROLL):
            c = gi * ATTN_UNROLL + u
            kc = k_ref[pl.ds(pl.multiple_of(c * tk, tk), tk), :]
            s = lax.dot_general(kc, q, (((1,), (1,)), ((), ())), preferred_element_type=F32)
            s_cur[c] = s
            m = jnp.maximum(m, jnp.max(s, axis=0, keepdims=True))
            p = jnp.exp2((s_prev[c] - m_old) * exp2_scale)
            acc_ref[...] += _dot(vt_ref[c], p.astype(BF16))
        return m

    m_cur[...] = lax.fori_loop(0, n_chunks // ATTN_UNROLL, body, jnp.full((1, q.shape[0]), -jnp.inf, F32))
    return acc_ref[:dv, :] / acc_ref[dv:dv + 1, :]


def _pipelined_steps(load_q, k_ref, vt_ref, bufs, acc_ref, exp2_scale, tk, emit):
    s0, s1, m0, m1 = bufs
    step = pl.program_id(2)

    @pl.when(step % 2 == 0)
    def _():
        emit(_attention_step(load_q(), k_ref, vt_ref, s0, s1, m0, m1, acc_ref, exp2_scale, tk))

    @pl.when(step % 2 == 1)
    def _():
        emit(_attention_step(load_q(), k_ref, vt_ref, s1, s0, m1, m0, acc_ref, exp2_scale, tk))


def _start_sequence(v_ref, vt_ref, bufs, tk):
    _, s1, _, m1 = bufs
    dv = v_ref.shape[1]
    for c in range(vt_ref.shape[0]):
        vt_ref[c, :dv, :] = v_ref[c * tk:(c + 1) * tk, :].astype(F32).T.astype(BF16)
        vt_ref[c, dv:, :] = jnp.ones((ONES_ROWS, tk), BF16)
    s1[...] = jnp.zeros_like(s1)
    m1[...] = jnp.zeros_like(m1)


def _attention_scratch(dv, m_cols, tk):
    n_chunks = SEQ_LEN // tk
    return [pltpu.VMEM((n_chunks, dv + ONES_ROWS, tk), BF16),
            pltpu.VMEM((n_chunks, tk, m_cols), F32), pltpu.VMEM((n_chunks, tk, m_cols), F32),
            pltpu.VMEM((1, m_cols), F32), pltpu.VMEM((1, m_cols), F32),
            pltpu.VMEM((dv + ONES_ROWS, m_cols), F32)]


def _gqa_kernel(q_ref, k_ref, v_ref, o_ref, vt_ref, s0, s1, m0, m1, acc_ref, *, exp2_scale, tk, group):
    tq = q_ref.shape[0]
    bufs = (s0, s1, m0, m1)

    @pl.when(pl.program_id(2) == 0)
    def _():
        _start_sequence(v_ref, vt_ref, bufs, tk)

    def load_q():
        return jnp.concatenate([q_ref[:, h * HD_B:(h + 1) * HD_B] for h in range(group)], axis=0)

    def emit(ot):
        for h in range(group):
            o_ref[:, h * HD_B:(h + 1) * HD_B] = ot[:, h * tq:(h + 1) * tq].T.astype(o_ref.dtype)

    _pipelined_steps(load_q, k_ref, vt_ref, bufs, acc_ref, exp2_scale, tk, emit)


def _gqa_attention(qk, v, n_batch, tq, tk):
    n = qk.shape[0]
    group = H_B // KV_B
    qt = SEQ_LEN // tq
    m_cols = group * tq
    vmem = (2 * (2 * tq * group * HD_B * 2 + 2 * SEQ_LEN * HD_B * 2) + SEQ_LEN * HD_B * 2
            + 2 * m_cols * SEQ_LEN * 4 + HD_B * m_cols * 4 + (16 << 20))
    return pl.pallas_call(
        functools.partial(_gqa_kernel, exp2_scale=HD_B ** -0.5 * LOG2_E, tk=tk, group=group),
        grid=(n_batch, KV_B, qt + 1),
        in_specs=[pl.BlockSpec((tq, group * HD_B), lambda b, g, i: (b * qt + jnp.minimum(i, qt - 1), g)),
                  pl.BlockSpec((SEQ_LEN, HD_B), lambda b, g, i: (b, H_B + g)),
                  pl.BlockSpec((SEQ_LEN, HD_B), lambda b, g, i: (b, g))],
        out_specs=pl.BlockSpec((tq, group * HD_B), lambda b, g, i: (b * qt + jnp.maximum(i - 1, 0), g)),
        out_shape=jax.ShapeDtypeStruct((n, H_B * HD_B), BF16),
        scratch_shapes=_attention_scratch(HD_B, m_cols, tk),
        compiler_params=_cparams(vmem),
        name="gqa_attention",
    )(qk, qk, v)


def _mla_kernel(q_ref, kn_ref, kr_ref, v_ref, o_ref, kfull_ref, vt_ref, s0, s1, m0, m1, acc_ref, *,
                exp2_scale, tk):
    bufs = (s0, s1, m0, m1)

    @pl.when(pl.program_id(2) == 0)
    def _():
        kfull_ref[:, :NOPE_A] = kn_ref[...]
        kfull_ref[:, NOPE_A:] = kr_ref[...]
        _start_sequence(v_ref, vt_ref, bufs, tk)

    def emit(ot):
        o_ref[...] = ot.T.astype(o_ref.dtype)

    _pipelined_steps(lambda: q_ref[...], kfull_ref, vt_ref, bufs, acc_ref, exp2_scale, tk, emit)


def _mla_attention(qa, kv, kr, n_batch, tq, tk):
    n = qa.shape[0]
    qt = SEQ_LEN // tq
    vmem = (2 * (tq * HEAD_A_PAD * 2 + 3 * SEQ_LEN * LANES * 2 + tq * V_A * 2)
            + SEQ_LEN * HEAD_A_PAD * 2 + SEQ_LEN * V_A * 2 + 2 * tq * SEQ_LEN * 4 + V_A * tq * 4 + (16 << 20))
    return pl.pallas_call(
        functools.partial(_mla_kernel, exp2_scale=(NOPE_A + ROPE_A) ** -0.5 * LOG2_E, tk=tk),
        grid=(n_batch, H_A, qt + 1),
        in_specs=[pl.BlockSpec((tq, HEAD_A_PAD), lambda b, h, i: (b * qt + jnp.minimum(i, qt - 1), h)),
                  pl.BlockSpec((SEQ_LEN, NOPE_A), lambda b, h, i: (b, 2 * h)),
                  pl.BlockSpec((SEQ_LEN, LANES), lambda b, h, i: (b, 0)),
                  pl.BlockSpec((SEQ_LEN, V_A), lambda b, h, i: (b, 2 * h + 1))],
        out_specs=pl.BlockSpec((tq, V_A), lambda b, h, i: (b * qt + jnp.maximum(i - 1, 0), h)),
        out_shape=jax.ShapeDtypeStruct((n, H_A * V_A), BF16),
        scratch_shapes=[pltpu.VMEM((SEQ_LEN, HEAD_A_PAD), BF16)] + _attention_scratch(V_A, tq, tk),
        compiler_params=_cparams(vmem),
        name="mla_attention",
    )(qa, kv, kr, kv)


def _router_kernel(x_ref, w_ref, tri_ref, idx_ref, wgt_ref, cnt_ref, run_ref):
    @pl.when(pl.program_id(0) == 0)
    def _():
        run_ref[...] = jnp.zeros_like(run_ref)

    logits = _dot(x_ref[...], w_ref[...])
    lane = lax.broadcasted_iota(I32, logits.shape, 1)
    neg = jnp.float32(-jnp.inf)
    logits = jnp.where(lane < N_EXPERTS, logits, neg)
    v1 = jnp.max(logits, axis=-1, keepdims=True)
    i1 = jnp.min(jnp.where(logits == v1, lane, LANES), axis=-1, keepdims=True)
    rest = jnp.where(lane == i1, neg, logits)
    v2 = jnp.max(rest, axis=-1, keepdims=True)
    i2 = jnp.min(jnp.where(rest == v2, lane, LANES), axis=-1, keepdims=True)
    e2 = jnp.exp(v2 - v1)
    denom = 1.0 + e2
    wgt_ref[...] = jnp.where(lane == 0, 1.0 / denom, 0.0) + jnp.where(lane == 1, e2 / denom, 0.0)

    hit1 = lane == i1
    hit2 = lane == i2
    onehot = jnp.where(hit1 | hit2, 1.0, 0.0)
    before = _dot(tri_ref[...], onehot.astype(BF16)) + run_ref[...]
    r1 = jnp.sum(jnp.where(hit1, before, 0.0), axis=-1, keepdims=True)
    r2 = jnp.sum(jnp.where(hit2, before, 0.0), axis=-1, keepdims=True)
    idx_ref[...] = (jnp.where(lane == 0, i1, 0) + jnp.where(lane == 1, i2, 0)
                    + jnp.where(lane == 2, r1.astype(I32), 0) + jnp.where(lane == 3, r2.astype(I32), 0))
    run_ref[...] += jnp.sum(onehot, axis=0, keepdims=True)
    cnt_ref[...] = jnp.broadcast_to(run_ref[...], cnt_ref.shape)


def _router(x, w_router_pad, tri):
    m, k = x.shape
    bm = tri.shape[0]
    vmem = 2 * (bm * k * 2 + k * LANES * 2 + bm * bm * 2 + 2 * bm * LANES * 4) + (12 << 20)
    return pl.pallas_call(
        _router_kernel,
        grid=(m // bm,),
        in_specs=[pl.BlockSpec((bm, k), lambda i: (i, 0)),
                  pl.BlockSpec((k, LANES), lambda i: (0, 0)),
                  pl.BlockSpec((bm, bm), lambda i: (0, 0))],
        out_specs=[pl.BlockSpec((bm, LANES), lambda i: (i, 0)),
                   pl.BlockSpec((bm, LANES), lambda i: (i, 0)),
                   pl.BlockSpec((8, LANES), lambda i: (0, 0))],
        out_shape=[jax.ShapeDtypeStruct((m, LANES), I32),
                   jax.ShapeDtypeStruct((m, LANES), F32),
                   jax.ShapeDtypeStruct((8, LANES), F32)],
        scratch_shapes=[pltpu.VMEM((1, LANES), F32)],
        compiler_params=_cparams(vmem),
        name="moe_router",
    )(x, w_router_pad, tri)


def _row_copy(src, src_row, dst, dst_row, sem):
    return pltpu.make_async_copy(src.at[pl.ds(src_row, 1)], dst.at[pl.ds(dst_row, 1)], sem)


def _dispatch_kernel(src_ref, x_hbm, o_ref, buf_ref, sem, *, bm):
    def issue(r, carry):
        _row_copy(x_hbm, src_ref[0, 0, r], buf_ref, r, sem).start()
        return carry

    lax.fori_loop(0, bm, issue, 0)

    def drain(r, carry):
        _row_copy(x_hbm, src_ref[0, 0, r], buf_ref, r, sem).wait()
        return carry

    lax.fori_loop(0, bm, drain, 0)
    o_ref[...] = buf_ref[...].astype(o_ref.dtype)


def _dispatch(x, src, bm):
    n, d = x.shape
    p = src.shape[0]
    vmem = 2 * bm * d * 2 + 2 * bm * d * 4 + (4 << 20)
    return pl.pallas_call(
        functools.partial(_dispatch_kernel, bm=bm),
        grid=(p // bm,),
        in_specs=[pl.BlockSpec((1, 1, bm), lambda i: (i, 0, 0), memory_space=pltpu.SMEM),
                  pl.BlockSpec(memory_space=pl.ANY)],
        out_specs=pl.BlockSpec((bm, d), lambda i: (i, 0)),
        out_shape=jax.ShapeDtypeStruct((p, d), BF16),
        scratch_shapes=[pltpu.VMEM((bm, d), F32), pltpu.SemaphoreType.DMA],
        compiler_params=_cparams(vmem),
        name="moe_dispatch",
    )(src.reshape(p // bm, 1, bm), x)


def _grouped_col(i, j, used, n_col):
    return jnp.where(i < used[0], j, n_col - 1)


def _moe_swiglu_kernel(te_ref, tr_ref, used_ref, x_ref, wg_ref, wu_ref, o_ref):
    live = pl.program_id(0) < used_ref[0]

    @pl.when(live)
    def _():
        x = x_ref[...]
        o_ref[...] = (jax.nn.silu(_dot(x, wg_ref[...])) * _dot(x, wu_ref[...])).astype(o_ref.dtype)

    @pl.when(jnp.logical_not(live))
    def _():
        o_ref[...] = jnp.zeros_like(o_ref)


def _moe_swiglu(xs, wg, wu, tile_expert, tile_row, n_used, bn):
    p, k = xs.shape
    n = wg.shape[2]
    bm = MOE_BM
    nc = n // bn
    vmem = 2 * (bm * k * 2 + 2 * k * bn * 2 + bm * bn * 2) + 3 * bm * bn * 4 + (4 << 20)
    return pl.pallas_call(
        _moe_swiglu_kernel,
        grid_spec=pltpu.PrefetchScalarGridSpec(
            num_scalar_prefetch=3,
            grid=(p // bm, nc),
            in_specs=[pl.BlockSpec((bm, k), lambda i, j, te, tr, u: (tr[i], 0)),
                      pl.BlockSpec((None, k, bn), lambda i, j, te, tr, u: (te[i], 0, _grouped_col(i, j, u, nc))),
                      pl.BlockSpec((None, k, bn), lambda i, j, te, tr, u: (te[i], 0, _grouped_col(i, j, u, nc)))],
            out_specs=pl.BlockSpec((bm, bn), lambda i, j, te, tr, u: (i, j)),
        ),
        out_shape=jax.ShapeDtypeStruct((p, n), BF16),
        compiler_params=_cparams(vmem),
        name="moe_swiglu",
    )(tile_expert, tile_row, n_used, xs, wg, wu)


def _moe_down_kernel(te_ref, tr_ref, used_ref, h_ref, w_ref, o_ref):
    live = pl.program_id(0) < used_ref[0]

    @pl.when(live)
    def _():
        o_ref[...] = _dot(h_ref[...], w_ref[...])

    @pl.when(jnp.logical_not(live))
    def _():
        o_ref[...] = jnp.zeros_like(o_ref)


def _moe_down(h, wd, tile_expert, tile_row, n_used, bn):
    p, k = h.shape
    n = wd.shape[2]
    bm = MOE_BM
    nc = n // bn
    vmem = 2 * (bm * k * 2 + k * bn * 2 + bm * bn * 4) + bm * bn * 4 + (4 << 20)
    return pl.pallas_call(
        _moe_down_kernel,
        grid_spec=pltpu.PrefetchScalarGridSpec(
            num_scalar_prefetch=3,
            grid=(p // bm, nc),
            in_specs=[pl.BlockSpec((bm, k), lambda i, j, te, tr, u: (tr[i], 0)),
                      pl.BlockSpec((None, k, bn), lambda i, j, te, tr, u: (te[i], 0, _grouped_col(i, j, u, nc)))],
            out_specs=pl.BlockSpec((bm, bn), lambda i, j, te, tr, u: (i, j)),
        ),
        out_shape=jax.ShapeDtypeStruct((p, n), F32),
        compiler_params=_cparams(vmem),
        name="moe_down",
    )(tile_expert, tile_row, n_used, h, wd)


def _combine_ln_kernel(pos_ref, x_ref, wgt_ref, g_ref, b_ref, y_hbm, o_ref, ob_ref, ybuf_ref, sem, *, bm):
    def issue(r, carry):
        _row_copy(y_hbm, pos_ref[0, 0, 2 * r], ybuf_ref.at[0], r, sem).start()
        _row_copy(y_hbm, pos_ref[0, 0, 2 * r + 1], ybuf_ref.at[1], r, sem).start()
        return carry

    lax.fori_loop(0, bm, issue, 0)

    def drain(r, carry):
        _row_copy(y_hbm, pos_ref[0, 0, 2 * r], ybuf_ref.at[0], r, sem).wait()
        _row_copy(y_hbm, pos_ref[0, 0, 2 * r + 1], ybuf_ref.at[1], r, sem).wait()
        return carry

    lax.fori_loop(0, bm, drain, 0)
    wgt = wgt_ref[...]
    f = wgt[:, 0:1] * ybuf_ref[0] + wgt[:, 1:2] * ybuf_ref[1]
    y = _layernorm(DN_ALPHA * x_ref[...] + f, g_ref[...], b_ref[...])
    o_ref[...] = y
    ob_ref[...] = y.astype(BF16)


def _combine_ln(x, y_sorted, pos, wgt, g, b, bm):
    n, d = x.shape
    pos3 = pos.reshape(n // bm, 1, 2 * bm)
    vmem = 2 * (bm * d * 4 + bm * LANES * 4 + bm * d * 4 + bm * d * 2) + 2 * bm * d * 4 + 3 * bm * d * 4 + (4 << 20)
    return pl.pallas_call(
        functools.partial(_combine_ln_kernel, bm=bm),
        grid=(n // bm,),
        in_specs=[pl.BlockSpec((1, 1, 2 * bm), lambda i: (i, 0, 0), memory_space=pltpu.SMEM),
                  pl.BlockSpec((bm, d), lambda i: (i, 0)),
                  pl.BlockSpec((bm, LANES), lambda i: (i, 0)),
                  pl.BlockSpec((1, d), lambda i: (0, 0)),
                  pl.BlockSpec((1, d), lambda i: (0, 0)),
                  pl.BlockSpec(memory_space=pl.ANY)],
        out_specs=[pl.BlockSpec((bm, d), lambda i: (i, 0)),
                   pl.BlockSpec((bm, d), lambda i: (i, 0))],
        out_shape=[jax.ShapeDtypeStruct((n, d), F32), jax.ShapeDtypeStruct((n, d), BF16)],
        scratch_shapes=[pltpu.VMEM((2, bm, d), F32), pltpu.SemaphoreType.DMA],
        compiler_params=_cparams(vmem),
        name="moe_combine_ln",
    )(pos3, x, wgt, g.reshape(1, d), b.reshape(1, d), y_sorted)


def _moe_layer(x, xb, w_router, wg, wu, wd, ln_g, ln_b, tri):
    n = x.shape[0]
    n_tiles = (2 * n) // MOE_BM + N_EXPERTS
    idx, wgt, cnt = _router(xb, w_router, tri)
    counts = cnt[0, :N_EXPERTS].astype(I32)
    tiles_per = (counts + MOE_BM - 1) // MOE_BM
    tile_end = jnp.cumsum(tiles_per)
    offsets = (tile_end - tiles_per) * MOE_BM
    tile_id = jnp.arange(n_tiles, dtype=I32)
    last_tile = tile_end[-1] - 1
    tile_row = jnp.minimum(tile_id, last_tile)
    tile_expert = jnp.minimum(jnp.sum(tile_row[:, None] >= tile_end[None, :], axis=1), N_EXPERTS - 1).astype(I32)
    order = jnp.argsort(idx[:, 0:2].reshape(-1), stable=True).astype(I32)
    group_start = jnp.cumsum(counts) - counts
    rank = jnp.arange(MOE_BM, dtype=I32)[None, :] + ((tile_row - (tile_end - tiles_per)[tile_expert]) * MOE_BM)[:, None]
    valid = (rank < counts[tile_expert][:, None]) & (tile_id <= last_tile)[:, None]
    slot = jnp.clip(group_start[tile_expert][:, None] + rank, 0, 2 * n - 1)
    src = jnp.where(valid, order[slot] // 2, 0).reshape(-1).astype(I32)
    expert_ids = jnp.arange(N_EXPERTS, dtype=I32)
    off1 = jnp.sum(jnp.where(idx[:, 0:1] == expert_ids[None, :], offsets[None, :], 0), axis=1)
    off2 = jnp.sum(jnp.where(idx[:, 1:2] == expert_ids[None, :], offsets[None, :], 0), axis=1)
    pos = jnp.stack([off1 + idx[:, 2], off2 + idx[:, 3]], axis=1).astype(I32)

    xs = _dispatch(x, src, MOE_BM)
    n_used = tile_end[-1:].astype(I32)
    hdn = _moe_swiglu(xs, wg, wu, tile_expert, tile_row, n_used, bn=512)
    y_sorted = _moe_down(hdn, wd, tile_expert, tile_row, n_used, bn=1024)
    return _combine_ln(x, y_sorted, pos, wgt, ln_g, ln_b, COMBINE_BM)


def _rope_tables(dim):
    seg = dim // 2
    inv = ROPE_THETA ** (-jnp.arange(0, seg, 2, dtype=F32) / seg)
    t = jnp.arange(SEQ_LEN)
    row = (t // GRID_W).astype(F32)
    col = (t % GRID_W).astype(F32)
    ar = row[:, None] * inv[None, :]
    ac = col[:, None] * inv[None, :]
    ar = jnp.concatenate([ar, ar], axis=-1)
    ac = jnp.concatenate([ac, ac], axis=-1)
    cos = jnp.concatenate([jnp.cos(ar), jnp.cos(ac)], axis=-1)
    sin = jnp.concatenate([jnp.sin(ar), jnp.sin(ac)], axis=-1)
    half = seg // 2
    sign = jnp.where((jnp.arange(dim) % seg) < half, -1.0, 1.0).astype(F32)
    sin = sin * sign[None, :]
    pad = LANES - dim
    if pad:
        cos = jnp.concatenate([cos, jnp.zeros((SEQ_LEN, pad), F32)], axis=-1)
        sin = jnp.concatenate([sin, jnp.zeros((SEQ_LEN, pad), F32)], axis=-1)
    return cos, sin


def kernel(x_prompt, x_sample, w_in, gate_bias, q_a_norm, w_q_b, kv_a_norm, w_kv_b, q_norm, k_norm, w_br_a,
           w_br_b, w_o, ln1_g, ln1_b, ffn_w_gate, ffn_w_up, ffn_w_down, moe_w_router, moe_w_gate, moe_w_up,
           moe_w_down, ln2_g, ln2_b):
    n_prompt, n_sample = x_prompt.shape[0], x_sample.shape[0]
    n_batch = n_prompt + n_sample
    x = jnp.concatenate([x_prompt, x_sample], axis=0).reshape(n_batch * SEQ_LEN, D_MODEL)
    xb = x.astype(BF16)

    cos_a, sin_a = _rope_tables(ROPE_A)
    cos_b, sin_b = _rope_tables(HD_B)
    tri = jnp.tril(jnp.ones((ROUTE_BM, ROUTE_BM), BF16), k=-1)

    c0 = Q_LORA
    c1 = c0 + KV_LORA
    c2 = c1 + ROPE_A
    c3 = c2 + H_B * HD_B
    c4 = c3 + KV_B * HD_B
    c5 = c4 + KV_B * HD_B

    for l in range(DEPTH):
        wl = w_in[l]
        w_lat = wl[:, :c1 + LANES].astype(BF16)
        w_rest = wl[:, c2:].astype(BF16)
        wq = w_q_b[l].reshape(Q_LORA, H_A, NOPE_A + ROPE_A)
        wq = jnp.pad(wq, ((0, 0), (0, 0), (0, HEAD_A_PAD - NOPE_A - ROPE_A)))
        wq = wq.reshape(Q_LORA, H_A * HEAD_A_PAD).astype(BF16)
        w_kvb = w_kv_b[l].astype(BF16)
        g_qk = jnp.concatenate([jnp.tile(q_norm[l], H_B), jnp.tile(k_norm[l], KV_B)]).reshape(1, -1)
        bias = gate_bias[l].reshape(1, 2 * D_MODEL)

        q_lat = _mm_rms(xb, w_lat, q_a_norm[l], bm=1024, name="q_latent")
        kv_lat = _mm_rms(xb, w_lat, kv_a_norm[l], bm=1024, name="kv_latent", col0=c0 // KV_LORA)
        k_rope = _mm_rope(xb, w_lat, cos_a, sin_a, bm=1024, bn=LANES, half=ROPE_A // 4, rope_groups=1,
                          name="k_rope", n=LANES, col0=c1 // LANES)
        qk_b = _mm_headnorm_rope(xb, w_rest, g_qk, cos_b, sin_b, bm=1024, bn=512, half=HD_B // 4, name="qk_b")
        v_b = _mm_plain(xb, w_rest, BF16, bm=1024, bn=512, name="v_b", n=c5 - c4, col0=(c4 - c2) // 512)
        gates = _mm_bias_sigmoid(xb, w_rest, bias, bm=1024, bn=1024, name="gates", col0=(c5 - c2) // 1024)

        qa = _mm_rope(q_lat, wq, cos_a, sin_a, bm=1024, bn=1024, half=ROPE_A // 4, rope_groups=2, name="q_a")
        kv = _mm_plain(kv_lat, w_kvb, BF16, bm=1024, bn=1024, name="kv_a")
        o_a = _mla_attention(qa, kv, k_rope, n_batch, tq=512, tk=512)
        o_b = _gqa_attention(qk_b, v_b, n_batch, tq=128, tk=512)

        y = _mm_gated_merge(o_a, o_b, w_br_a[l].astype(BF16), w_br_b[l].astype(BF16), gates, bm=1024, bn=512,
                            name="gated_merge")
        mix = _mm_plain(y, w_o[l].astype(BF16), F32, bm=1024, bn=1024, name="w_o")
        x, xb = _ln_residual(x, mix, ln1_g[l], ln1_b[l], bm=256, name="ln1")

        i = l // 2
        if l % 2 == 0:
            pad = -D_FF_DENSE % DOWN_BK
            wg = jnp.pad(ffn_w_gate[i], ((0, 0), (0, pad))).astype(BF16)
            wu = jnp.pad(ffn_w_up[i], ((0, 0), (0, pad))).astype(BF16)
            wd = jnp.pad(ffn_w_down[i], ((0, pad), (0, 0))).astype(BF16).reshape(-1, DOWN_BK, D_MODEL)
            hdn = _mm_swiglu(xb, wg, wu, bm=1024, bn=512, name="ffn_swiglu")
            f = _mm_down(hdn, wd, bm=1024, bn=1024, name="ffn_down")
            x, xb = _ln_residual(x, f, ln2_g[l], ln2_b[l], bm=256, name="ln2")
        else:
            w_r = jnp.pad(moe_w_router[i], ((0, 0), (0, LANES - N_EXPERTS))).astype(BF16)
            x, xb = _moe_layer(x, xb, w_r, moe_w_gate[i].astype(BF16), moe_w_up[i].astype(BF16),
                               moe_w_down[i].astype(BF16), ln2_g[l], ln2_b[l], tri)

    y = x.reshape(n_batch, SEQ_LEN, D_MODEL)
    return y[:n_prompt], y[n_prompt:]
```

```python
import functools
import math

import jax
import jax.numpy as jnp
from jax import lax
from jax.experimental import pallas as pl
from jax.experimental.pallas import tpu as pltpu

F32 = jnp.float32
BF16 = jnp.bfloat16
I32 = jnp.int32

D_MODEL = 4096
SEQ_LEN = 4096
DEPTH = 4
GRID_W = 64
ROPE_THETA = 10000.0
NORM_EPS = 1e-6
LN_EPS = 1e-5
H_A, Q_LORA, KV_LORA, NOPE_A, ROPE_A, V_A = 16, 1024, 512, 128, 64, 128
H_B, KV_B, HD_B = 16, 4, 128
D_FF_DENSE = 11008
D_FF_EXPERT = 4096
N_EXPERTS = 8
DN_ALPHA = (2 * DEPTH) ** 0.25

LANES = 128
VMEM_BUDGET = 56 * 1024 * 1024

DOWN_BK = 1024
HEAD_A_PAD = 2 * LANES
MOE_BM = 512
ROUTE_BM = 1024
COMBINE_BM = 256
LOG2_E = math.log2(math.e)
ONES_ROWS = 16
ATTN_UNROLL = 4
EPILOGUE_ROWS = 256


def _cparams(vmem_bytes):
    return pltpu.CompilerParams(vmem_limit_bytes=int(min(VMEM_BUDGET, vmem_bytes)))


def _dot(a, b):
    return jnp.dot(a, b, preferred_element_type=F32)


def _rope_lanes(y, cos, sin_signed, half):
    lane = lax.broadcasted_iota(I32, y.shape, y.ndim - 1)
    lo = (lane % (2 * half)) < half
    partner = jnp.where(lo, pltpu.roll(y, LANES - half, y.ndim - 1), pltpu.roll(y, half, y.ndim - 1))
    return y * cos + partner * sin_signed


def _layernorm(z, g, b):
    mu = jnp.mean(z, axis=-1, keepdims=True)
    zc = z - mu
    var = jnp.mean(zc * zc, axis=-1, keepdims=True)
    return zc * lax.rsqrt(var + LN_EPS) * g + b


def _mm_plain_kernel(x_ref, w_ref, o_ref):
    o_ref[...] = _dot(x_ref[...], w_ref[...]).astype(o_ref.dtype)


def _mm_plain(x, w, out_dtype, bm, bn, name, n=None, col0=0):
    m, k = x.shape
    n = w.shape[1] if n is None else n
    ob = jnp.dtype(out_dtype).itemsize
    vmem = 2 * (bm * k * 2 + k * bn * 2 + bm * bn * ob) + bm * bn * 4 + (4 << 20)
    return pl.pallas_call(
        _mm_plain_kernel,
        grid=(m // bm, n // bn),
        in_specs=[pl.BlockSpec((bm, k), lambda i, j: (i, 0)),
                  pl.BlockSpec((k, bn), lambda i, j: (0, j + col0))],
        out_specs=pl.BlockSpec((bm, bn), lambda i, j: (i, j)),
        out_shape=jax.ShapeDtypeStruct((m, n), out_dtype),
        compiler_params=_cparams(vmem),
        name=name,
    )(x, w)


def _mm_rms_kernel(x_ref, w_ref, g_ref, o_ref):
    h = _dot(x_ref[...], w_ref[...])
    y = h * lax.rsqrt(jnp.mean(h * h, axis=-1, keepdims=True) + NORM_EPS)
    o_ref[...] = (y * g_ref[...]).astype(o_ref.dtype)


def _mm_rms(x, w, g, bm, name, col0=0):
    m, k = x.shape
    n = g.shape[0]
    vmem = 2 * (bm * k * 2 + k * n * 2 + bm * n * 2) + 2 * bm * n * 4 + (4 << 20)
    return pl.pallas_call(
        _mm_rms_kernel,
        grid=(m // bm,),
        in_specs=[pl.BlockSpec((bm, k), lambda i: (i, 0)),
                  pl.BlockSpec((k, n), lambda i: (0, col0)),
                  pl.BlockSpec((1, n), lambda i: (0, 0))],
        out_specs=pl.BlockSpec((bm, n), lambda i: (i, 0)),
        out_shape=jax.ShapeDtypeStruct((m, n), BF16),
        compiler_params=_cparams(vmem),
        name=name,
    )(x, w, g.reshape(1, n).astype(F32))


def _mm_rope_kernel(x_ref, w_ref, cos_ref, sin_ref, o_ref, *, half, rope_groups):
    h = _dot(x_ref[...], w_ref[...])
    n_groups = h.shape[1] // LANES
    for gi in range(n_groups):
        hg = h[:, gi * LANES:(gi + 1) * LANES]
        if gi % rope_groups == rope_groups - 1:
            hg = _rope_lanes(hg, cos_ref[...], sin_ref[...], half)
        o_ref[:, gi * LANES:(gi + 1) * LANES] = hg.astype(o_ref.dtype)


def _mm_rope(x, w, cos, sin, bm, bn, half, rope_groups, name, n=None, col0=0):
    m, k = x.shape
    n = w.shape[1] if n is None else n
    t_blocks = SEQ_LEN // bm
    vmem = 2 * (bm * k * 2 + k * bn * 2 + bm * bn * 2 + 2 * bm * LANES * 4) + 2 * bm * bn * 4 + (4 << 20)
    return pl.pallas_call(
        functools.partial(_mm_rope_kernel, half=half, rope_groups=rope_groups),
        grid=(m // bm, n // bn),
        in_specs=[pl.BlockSpec((bm, k), lambda i, j: (i, 0)),
                  pl.BlockSpec((k, bn), lambda i, j: (0, j + col0)),
                  pl.BlockSpec((bm, LANES), lambda i, j: (i % t_blocks, 0)),
                  pl.BlockSpec((bm, LANES), lambda i, j: (i % t_blocks, 0))],
        out_specs=pl.BlockSpec((bm, bn), lambda i, j: (i, j)),
        out_shape=jax.ShapeDtypeStruct((m, n), BF16),
        compiler_params=_cparams(vmem),
        name=name,
    )(x, w, cos, sin)


def _mm_headnorm_rope_kernel(x_ref, w_ref, g_ref, cos_ref, sin_ref, o_ref, *, half):
    for r0 in range(0, x_ref.shape[0], EPILOGUE_ROWS):
        rows = slice(r0, r0 + EPILOGUE_ROWS)
        h = _dot(x_ref[rows, :], w_ref[...])
        for gi in range(h.shape[1] // LANES):
            cols = slice(gi * LANES, (gi + 1) * LANES)
            hg = h[:, cols]
            y = hg * lax.rsqrt(jnp.mean(hg * hg, axis=-1, keepdims=True) + NORM_EPS)
            y = y * g_ref[:, cols]
            o_ref[rows, cols] = _rope_lanes(y, cos_ref[rows, :], sin_ref[rows, :], half).astype(o_ref.dtype)


def _mm_headnorm_rope(x, w, g, cos, sin, bm, bn, half, name):
    m, k = x.shape
    n = g.shape[1]
    t_blocks = SEQ_LEN // bm
    vmem = 2 * (bm * k * 2 + k * bn * 2 + bm * bn * 2 + 2 * bm * LANES * 4) + 2 * bm * bn * 4 + (4 << 20)
    return pl.pallas_call(
        functools.partial(_mm_headnorm_rope_kernel, half=half),
        grid=(m // bm, n // bn),
        in_specs=[pl.BlockSpec((bm, k), lambda i, j: (i, 0)),
                  pl.BlockSpec((k, bn), lambda i, j: (0, j)),
                  pl.BlockSpec((1, bn), lambda i, j: (0, j)),
                  pl.BlockSpec((bm, LANES), lambda i, j: (i % t_blocks, 0)),
                  pl.BlockSpec((bm, LANES), lambda i, j: (i % t_blocks, 0))],
        out_specs=pl.BlockSpec((bm, bn), lambda i, j: (i, j)),
        out_shape=jax.ShapeDtypeStruct((m, n), BF16),
        compiler_params=_cparams(vmem),
        name=name,
    )(x, w, g, cos, sin)


def _mm_bias_sigmoid_kernel(x_ref, w_ref, b_ref, o_ref):
    o_ref[...] = jax.nn.sigmoid(_dot(x_ref[...], w_ref[...]) + b_ref[...]).astype(o_ref.dtype)


def _mm_bias_sigmoid(x, w, b, bm, bn, name, col0=0):
    m, k = x.shape
    n = b.shape[1]
    vmem = 2 * (bm * k * 2 + k * bn * 2 + bm * bn * 2) + 2 * bm * bn * 4 + (4 << 20)
    return pl.pallas_call(
        _mm_bias_sigmoid_kernel,
        grid=(m // bm, n // bn),
        in_specs=[pl.BlockSpec((bm, k), lambda i, j: (i, 0)),
                  pl.BlockSpec((k, bn), lambda i, j: (0, j + col0)),
                  pl.BlockSpec((1, bn), lambda i, j: (0, j))],
        out_specs=pl.BlockSpec((bm, bn), lambda i, j: (i, j)),
        out_shape=jax.ShapeDtypeStruct((m, n), BF16),
        compiler_params=_cparams(vmem),
        name=name,
    )(x, w, b)


def _mm_gated_merge_kernel(oa_ref, ob_ref, wa_ref, wb_ref, ga_ref, gb_ref, o_ref):
    pa = _dot(oa_ref[...], wa_ref[...])
    pb = _dot(ob_ref[...], wb_ref[...])
    o_ref[...] = (ga_ref[...].astype(F32) * pa + gb_ref[...].astype(F32) * pb).astype(o_ref.dtype)


def _mm_gated_merge(oa, ob, wa, wb, gates, bm, bn, name):
    m, k = oa.shape
    n = wa.shape[1]
    nb = n // bn
    vmem = 2 * (2 * bm * k * 2 + 2 * k * bn * 2 + 3 * bm * bn * 2) + 3 * bm * bn * 4 + (4 << 20)
    return pl.pallas_call(
        _mm_gated_merge_kernel,
        grid=(m // bm, nb),
        in_specs=[pl.BlockSpec((bm, k), lambda i, j: (i, 0)),
                  pl.BlockSpec((bm, k), lambda i, j: (i, 0)),
                  pl.BlockSpec((k, bn), lambda i, j: (0, j)),
                  pl.BlockSpec((k, bn), lambda i, j: (0, j)),
                  pl.BlockSpec((bm, bn), lambda i, j: (i, j)),
                  pl.BlockSpec((bm, bn), lambda i, j: (i, j + nb))],
        out_specs=pl.BlockSpec((bm, bn), lambda i, j: (i, j)),
        out_shape=jax.ShapeDtypeStruct((m, n), BF16),
        compiler_params=_cparams(vmem),
        name=name,
    )(oa, ob, wa, wb, gates, gates)


def _mm_swiglu_kernel(x_ref, wg_ref, wu_ref, o_ref):
    x = x_ref[...]
    o_ref[...] = (jax.nn.silu(_dot(x, wg_ref[...])) * _dot(x, wu_ref[...])).astype(o_ref.dtype)


def _mm_swiglu(x, wg, wu, bm, bn, name):
    m, k = x.shape
    n = wg.shape[1]
    vmem = 2 * (bm * k * 2 + 2 * k * bn * 2 + bm * bn * 2) + 3 * bm * bn * 4 + (4 << 20)
    return pl.pallas_call(
        _mm_swiglu_kernel,
        grid=(m // bm, n // bn),
        in_specs=[pl.BlockSpec((bm, k), lambda i, j: (i, 0)),
                  pl.BlockSpec((k, bn), lambda i, j: (0, j)),
                  pl.BlockSpec((k, bn), lambda i, j: (0, j))],
        out_specs=pl.BlockSpec((bm, bn), lambda i, j: (i, j)),
        out_shape=jax.ShapeDtypeStruct((m, n), BF16),
        compiler_params=_cparams(vmem),
        name=name,
    )(x, wg, wu)


def _mm_down_kernel(h_ref, w_ref, o_ref, acc_ref, *, n_k):
    kk = pl.program_id(2)

    @pl.when(kk == 0)
    def _():
        acc_ref[...] = jnp.zeros_like(acc_ref)

    acc_ref[...] += _dot(h_ref[...], w_ref[...])

    @pl.when(kk == n_k - 1)
    def _():
        o_ref[...] = acc_ref[...]


def _mm_down(h, w, bm, bn, name):
    m = h.shape[0]
    n_k, bk, n = w.shape
    vmem = 2 * (bm * bk * 2 + bk * bn * 2 + bm * bn * 4) + 3 * bm * bn * 4 + (4 << 20)
    return pl.pallas_call(
        functools.partial(_mm_down_kernel, n_k=n_k),
        grid=(m // bm, n // bn, n_k),
        in_specs=[pl.BlockSpec((bm, bk), lambda i, j, k: (i, k)),
                  pl.BlockSpec((None, bk, bn), lambda i, j, k: (k, 0, j))],
        out_specs=pl.BlockSpec((bm, bn), lambda i, j, k: (i, j)),
        out_shape=jax.ShapeDtypeStruct((m, n), F32),
        scratch_shapes=[pltpu.VMEM((bm, bn), F32)],
        compiler_params=_cparams(vmem),
        name=name,
    )(h, w)


def _ln_kernel(x_ref, f_ref, g_ref, b_ref, o_ref, ob_ref):
    y = _layernorm(DN_ALPHA * x_ref[...] + f_ref[...], g_ref[...], b_ref[...])
    o_ref[...] = y
    ob_ref[...] = y.astype(BF16)


def _ln_residual(x, f, g, b, bm, name):
    m, d = x.shape
    vmem = 2 * (2 * bm * d * 4 + bm * d * 4 + bm * d * 2) + 3 * bm * d * 4 + (4 << 20)
    return pl.pallas_call(
        _ln_kernel,
        grid=(m // bm,),
        in_specs=[pl.BlockSpec((bm, d), lambda i: (i, 0)),
                  pl.BlockSpec((bm, d), lambda i: (i, 0)),
                  pl.BlockSpec((1, d), lambda i: (0, 0)),
                  pl.BlockSpec((1, d), lambda i: (0, 0))],
        out_specs=[pl.BlockSpec((bm, d), lambda i: (i, 0)),
                   pl.BlockSpec((bm, d), lambda i: (i, 0))],
        out_shape=[jax.ShapeDtypeStruct((m, d), F32), jax.ShapeDtypeStruct((m, d), BF16)],
        compiler_params=_cparams(vmem),
        name=name,
    )(x, f, g.reshape(1, d), b.reshape(1, d))


def _attention_step(q, k_ref, vt_ref, s_cur, s_prev, m_cur, m_prev, acc_ref, exp2_scale, tk):
    n_chunks = s_cur.shape[0]
    assert n_chunks % ATTN_UNROLL == 0
    dv = acc_ref.shape[0] - ONES_ROWS
    m_old = m_prev[...]
    acc_ref[...] = jnp.zeros_like(acc_ref)

    def body(gi, m):
        for u in range(ATTN_UNROLL):
            c = gi * ATTN_UNROLL + u
            kc = k_ref[pl.ds(pl.multiple_of(c * tk, tk), tk), :]
            s = lax.dot_general(kc, q, (((1,), (1,)), ((), ())), preferred_element_type=F32)
            s_cur[c] = s
            m = jnp.maximum(m, jnp.max(s, axis=0, keepdims=True))
            p = jnp.exp2((s_prev[c] - m_old) * exp2_scale)
            acc_ref[...] += _dot(vt_ref[c], p.astype(BF16))
        return m

    m_cur[...] = lax.fori_loop(0, n_chunks // ATTN_UNROLL, body, jnp.full((1, q.shape[0]), -jnp.inf, F32))
    return acc_ref[:dv, :] / acc_ref[dv:dv + 1, :]


def _pipelined_steps(load_q, k_ref, vt_ref, bufs, acc_ref, exp2_scale, tk, emit):
    s0, s1, m0, m1 = bufs
    step = pl.program_id(2)

    @pl.when(step % 2 == 0)
    def _():
        emit(_attention_step(load_q(), k_ref, vt_ref, s0, s1, m0, m1, acc_ref, exp2_scale, tk))

    @pl.when(step % 2 == 1)
    def _():
        emit(_attention_step(load_q(), k_ref, vt_ref, s1, s0, m1, m0, acc_ref, exp2_scale, tk))


def _start_sequence(v_ref, vt_ref, bufs, tk):
    _, s1, _, m1 = bufs
    dv = v_ref.shape[1]
    for c in range(vt_ref.shape[0]):
        vt_ref[c, :dv, :] = v_ref[c * tk:(c + 1) * tk, :].astype(F32).T.astype(BF16)
        vt_ref[c, dv:, :] = jnp.ones((ONES_ROWS, tk), BF16)
    s1[...] = jnp.zeros_like(s1)
    m1[...] = jnp.zeros_like(m1)


def _attention_scratch(dv, m_cols, tk):
    n_chunks = SEQ_LEN // tk
    return [pltpu.VMEM((n_chunks, dv + ONES_ROWS, tk), BF16),
            pltpu.VMEM((n_chunks, tk, m_cols), F32), pltpu.VMEM((n_chunks, tk, m_cols), F32),
            pltpu.VMEM((1, m_cols), F32), pltpu.VMEM((1, m_cols), F32),
            pltpu.VMEM((dv + ONES_ROWS, m_cols), F32)]


def _gqa_kernel(q_ref, k_ref, v_ref, o_ref, vt_ref, s0, s1, m0, m1, acc_ref, *, exp2_scale, tk, group):
    tq = q_ref.shape[0]
    bufs = (s0, s1, m0, m1)

    @pl.when(pl.program_id(2) == 0)
    def _():
        _start_sequence(v_ref, vt_ref, bufs, tk)

    def load_q():
        return jnp.concatenate([q_ref[:, h * HD_B:(h + 1) * HD_B] for h in range(group)], axis=0)

    def emit(ot):
        for h in range(group):
            o_ref[:, h * HD_B:(h + 1) * HD_B] = ot[:, h * tq:(h + 1) * tq].T.astype(o_ref.dtype)

    _pipelined_steps(load_q, k_ref, vt_ref, bufs, acc_ref, exp2_scale, tk, emit)


def _gqa_attention(qk, v, n_batch, tq, tk):
    n = qk.shape[0]
    group = H_B // KV_B
    qt = SEQ_LEN // tq
    m_cols = group * tq
    vmem = (2 * (2 * tq * group * HD_B * 2 + 2 * SEQ_LEN * HD_B * 2) + SEQ_LEN * HD_B * 2
            + 2 * m_cols * SEQ_LEN * 4 + HD_B * m_cols * 4 + (16 << 20))
    return pl.pallas_call(
        functools.partial(_gqa_kernel, exp2_scale=HD_B ** -0.5 * LOG2_E, tk=tk, group=group),
        grid=(n_batch, KV_B, qt + 1),
        in_specs=[pl.BlockSpec((tq, group * HD_B), lambda b, g, i: (b * qt + jnp.minimum(i, qt - 1), g)),
                  pl.BlockSpec((SEQ_LEN, HD_B), lambda b, g, i: (b, H_B + g)),
                  pl.BlockSpec((SEQ_LEN, HD_B), lambda b, g, i: (b, g))],
        out_specs=pl.BlockSpec((tq, group * HD_B), lambda b, g, i: (b * qt + jnp.maximum(i - 1, 0), g)),
        out_shape=jax.ShapeDtypeStruct((n, H_B * HD_B), BF16),
        scratch_shapes=_attention_scratch(HD_B, m_cols, tk),
        compiler_params=_cparams(vmem),
        name="gqa_attention",
    )(qk, qk, v)


def _mla_kernel(q_ref, kn_ref, kr_ref, v_ref, o_ref, kfull_ref, vt_ref, s0, s1, m0, m1, acc_ref, *,
                exp2_scale, tk):
    bufs = (s0, s1, m0, m1)

    @pl.when(pl.program_id(2) == 0)
    def _():
        kfull_ref[:, :NOPE_A] = kn_ref[...]
        kfull_ref[:, NOPE_A:] = kr_ref[...]
        _start_sequence(v_ref, vt_ref, bufs, tk)

    def emit(ot):
        o_ref[...] = ot.T.astype(o_ref.dtype)

    _pipelined_steps(lambda: q_ref[...], kfull_ref, vt_ref, bufs, acc_ref, exp2_scale, tk, emit)


def _mla_attention(qa, kv, kr, n_batch, tq, tk):
    n = qa.shape[0]
    qt = SEQ_LEN // tq
    vmem = (2 * (tq * HEAD_A_PAD * 2 + 3 * SEQ_LEN * LANES * 2 + tq * V_A * 2)
            + SEQ_LEN * HEAD_A_PAD * 2 + SEQ_LEN * V_A * 2 + 2 * tq * SEQ_LEN * 4 + V_A * tq * 4 + (16 << 20))
    return pl.pallas_call(
        functools.partial(_mla_kernel, exp2_scale=(NOPE_A + ROPE_A) ** -0.5 * LOG2_E, tk=tk),
        grid=(n_batch, H_A, qt + 1),
        in_specs=[pl.BlockSpec((tq, HEAD_A_PAD), lambda b, h, i: (b * qt + jnp.minimum(i, qt - 1), h)),
                  pl.BlockSpec((SEQ_LEN, NOPE_A), lambda b, h, i: (b, 2 * h)),
                  pl.BlockSpec((SEQ_LEN, LANES), lambda b, h, i: (b, 0)),
                  pl.BlockSpec((SEQ_LEN, V_A), lambda b, h, i: (b, 2 * h + 1))],
        out_specs=pl.BlockSpec((tq, V_A), lambda b, h, i: (b * qt + jnp.maximum(i - 1, 0), h)),
        out_shape=jax.ShapeDtypeStruct((n, H_A * V_A), BF16),
        scratch_shapes=[pltpu.VMEM((SEQ_LEN, HEAD_A_PAD), BF16)] + _attention_scratch(V_A, tq, tk),
        compiler_params=_cparams(vmem),
        name="mla_attention",
    )(qa, kv, kr, kv)


def _router_kernel(x_ref, w_ref, tri_ref, idx_ref, wgt_ref, cnt_ref, run_ref):
    @pl.when(pl.program_id(0) == 0)
    def _():
        run_ref[...] = jnp.zeros_like(run_ref)

    logits = _dot(x_ref[...], w_ref[...])
    lane = lax.broadcasted_iota(I32, logits.shape, 1)
    neg = jnp.float32(-jnp.inf)
    logits = jnp.where(lane < N_EXPERTS, logits, neg)
    v1 = jnp.max(logits, axis=-1, keepdims=True)
    i1 = jnp.min(jnp.where(logits == v1, lane, LANES), axis=-1, keepdims=True)
    rest = jnp.where(lane == i1, neg, logits)
    v2 = jnp.max(rest, axis=-1, keepdims=True)
    i2 = jnp.min(jnp.where(rest == v2, lane, LANES), axis=-1, keepdims=True)
    e2 = jnp.exp(v2 - v1)
    denom = 1.0 + e2
    wgt_ref[...] = jnp.where(lane == 0, 1.0 / denom, 0.0) + jnp.where(lane == 1, e2 / denom, 0.0)

    hit1 = lane == i1
    hit2 = lane == i2
    onehot = jnp.where(hit1 | hit2, 1.0, 0.0)
    before = _dot(tri_ref[...], onehot.astype(BF16)) + run_ref[...]
    r1 = jnp.sum(jnp.where(hit1, before, 0.0), axis=-1, keepdims=True)
    r2 = jnp.sum(jnp.where(hit2, before, 0.0), axis=-1, keepdims=True)
    idx_ref[...] = (jnp.where(lane == 0, i1, 0) + jnp.where(lane == 1, i2, 0)
                    + jnp.where(lane == 2, r1.astype(I32), 0) + jnp.where(lane == 3, r2.astype(I32), 0))
    run_ref[...] += jnp.sum(onehot, axis=0, keepdims=True)
    cnt_ref[...] = jnp.broadcast_to(run_ref[...], cnt_ref.shape)


def _router(x, w_router_pad, tri):
    m, k = x.shape
    bm = tri.shape[0]
    vmem = 2 * (bm * k * 2 + k * LANES * 2 + bm * bm * 2 + 2 * bm * LANES * 4) + (12 << 20)
    return pl.pallas_call(
        _router_kernel,
        grid=(m // bm,),
        in_specs=[pl.BlockSpec((bm, k), lambda i: (i, 0)),
                  pl.BlockSpec((k, LANES), lambda i: (0, 0)),
                  pl.BlockSpec((bm, bm), lambda i: (0, 0))],
        out_specs=[pl.BlockSpec((bm, LANES), lambda i: (i, 0)),
                   pl.BlockSpec((bm, LANES), lambda i: (i, 0)),
                   pl.BlockSpec((8, LANES), lambda i: (0, 0))],
        out_shape=[jax.ShapeDtypeStruct((m, LANES), I32),
                   jax.ShapeDtypeStruct((m, LANES), F32),
                   jax.ShapeDtypeStruct((8, LANES), F32)],
        scratch_shapes=[pltpu.VMEM((1, LANES), F32)],
        compiler_params=_cparams(vmem),
        name="moe_router",
    )(x, w_router_pad, tri)


def _row_copy(src, src_row, dst, dst_row, sem):
    return pltpu.make_async_copy(src.at[pl.ds(src_row, 1)], dst.at[pl.ds(dst_row, 1)], sem)


def _gather_rows(src_hbm, idx_ref, n_rows, stride, offset, dst, sem, wait):
    def body(r, carry):
        copy = _row_copy(src_hbm, idx_ref[0, 0, stride * r + offset], dst, r, sem)
        if wait:
            copy.wait()
        else:
            copy.start()
        return carry

    lax.fori_loop(0, n_rows, body, 0, unroll=8)


def _dispatch_kernel(src_ref, nxt_ref, x_hbm, o_ref, buf_ref, sem, *, bm):
    step = pl.program_id(0)
    slot = step % 2

    @pl.when(step == 0)
    def _():
        _gather_rows(x_hbm, src_ref, bm, 1, 0, buf_ref.at[0], sem.at[0], wait=False)

    @pl.when(step + 1 < pl.num_programs(0))
    def _():
        _gather_rows(x_hbm, nxt_ref, bm, 1, 0, buf_ref.at[1 - slot], sem.at[1 - slot], wait=False)

    _gather_rows(x_hbm, src_ref, bm, 1, 0, buf_ref.at[slot], sem.at[slot], wait=True)
    o_ref[...] = buf_ref[slot].astype(o_ref.dtype)


def _dispatch(x, src, bm):
    n, d = x.shape
    p = src.shape[0]
    n_steps = p // bm
    src3 = src.reshape(n_steps, 1, bm)
    vmem = 2 * bm * d * 2 + 3 * bm * d * 4 + (4 << 20)
    return pl.pallas_call(
        functools.partial(_dispatch_kernel, bm=bm),
        grid=(n_steps,),
        in_specs=[pl.BlockSpec((1, 1, bm), lambda i: (i, 0, 0), memory_space=pltpu.SMEM),
                  pl.BlockSpec((1, 1, bm), lambda i: (jnp.minimum(i + 1, n_steps - 1), 0, 0),
                               memory_space=pltpu.SMEM),
                  pl.BlockSpec(memory_space=pl.ANY)],
        out_specs=pl.BlockSpec((bm, d), lambda i: (i, 0)),
        out_shape=jax.ShapeDtypeStruct((p, d), BF16),
        scratch_shapes=[pltpu.VMEM((2, bm, d), F32), pltpu.SemaphoreType.DMA((2,))],
        compiler_params=_cparams(vmem),
        name="moe_dispatch",
    )(src3, src3, x)


def _grouped_col(i, j, used, n_col):
    return jnp.where(i < used[0], j, n_col - 1)


def _moe_swiglu_kernel(te_ref, tr_ref, used_ref, x_ref, wg_ref, wu_ref, o_ref):
    live = pl.program_id(0) < used_ref[0]

    @pl.when(live)
    def _():
        x = x_ref[...]
        o_ref[...] = (jax.nn.silu(_dot(x, wg_ref[...])) * _dot(x, wu_ref[...])).astype(o_ref.dtype)

    @pl.when(jnp.logical_not(live))
    def _():
        o_ref[...] = jnp.zeros_like(o_ref)


def _moe_swiglu(xs, wg, wu, tile_expert, tile_row, n_used, bn):
    p, k = xs.shape
    n = wg.shape[2]
    bm = MOE_BM
    nc = n // bn
    vmem = 2 * (bm * k * 2 + 2 * k * bn * 2 + bm * bn * 2) + 3 * bm * bn * 4 + (4 << 20)
    return pl.pallas_call(
        _moe_swiglu_kernel,
        grid_spec=pltpu.PrefetchScalarGridSpec(
            num_scalar_prefetch=3,
            grid=(p // bm, nc),
            in_specs=[pl.BlockSpec((bm, k), lambda i, j, te, tr, u: (tr[i], 0)),
                      pl.BlockSpec((None, k, bn), lambda i, j, te, tr, u: (te[i], 0, _grouped_col(i, j, u, nc))),
                      pl.BlockSpec((None, k, bn), lambda i, j, te, tr, u: (te[i], 0, _grouped_col(i, j, u, nc)))],
            out_specs=pl.BlockSpec((bm, bn), lambda i, j, te, tr, u: (i, j)),
        ),
        out_shape=jax.ShapeDtypeStruct((p, n), BF16),
        compiler_params=_cparams(vmem),
        name="moe_swiglu",
    )(tile_expert, tile_row, n_used, xs, wg, wu)


def _moe_down_kernel(te_ref, tr_ref, used_ref, h_ref, w_ref, o_ref):
    live = pl.program_id(0) < used_ref[0]

    @pl.when(live)
    def _():
        o_ref[...] = _dot(h_ref[...], w_ref[...])

    @pl.when(jnp.logical_not(live))
    def _():
        o_ref[...] = jnp.zeros_like(o_ref)


def _moe_down(h, wd, tile_expert, tile_row, n_used, bn):
    p, k = h.shape
    n = wd.shape[2]
    bm = MOE_BM
    nc = n // bn
    vmem = 2 * (bm * k * 2 + k * bn * 2 + bm * bn * 4) + bm * bn * 4 + (4 << 20)
    return pl.pallas_call(
        _moe_down_kernel,
        grid_spec=pltpu.PrefetchScalarGridSpec(
            num_scalar_prefetch=3,
            grid=(p // bm, nc),
            in_specs=[pl.BlockSpec((bm, k), lambda i, j, te, tr, u: (tr[i], 0)),
                      pl.BlockSpec((None, k, bn), lambda i, j, te, tr, u: (te[i], 0, _grouped_col(i, j, u, nc)))],
            out_specs=pl.BlockSpec((bm, bn), lambda i, j, te, tr, u: (i, j)),
        ),
        out_shape=jax.ShapeDtypeStruct((p, n), F32),
        compiler_params=_cparams(vmem),
        name="moe_down",
    )(tile_expert, tile_row, n_used, h, wd)


def _combine_ln_kernel(pos_ref, nxt_ref, x_ref, wgt_ref, g_ref, b_ref, y_hbm, o_ref, ob_ref, ybuf_ref, sem, *, bm):
    step = pl.program_id(0)
    slot = step % 2

    def gather(idx_ref, s, wait):
        for k in range(2):
            _gather_rows(y_hbm, idx_ref, bm, 2, k, ybuf_ref.at[s, k], sem.at[s], wait)

    @pl.when(step == 0)
    def _():
        gather(pos_ref, 0, wait=False)

    @pl.when(step + 1 < pl.num_programs(0))
    def _():
        gather(nxt_ref, 1 - slot, wait=False)

    gather(pos_ref, slot, wait=True)
    wgt = wgt_ref[...]
    f = wgt[:, 0:1] * ybuf_ref[slot, 0] + wgt[:, 1:2] * ybuf_ref[slot, 1]
    y = _layernorm(DN_ALPHA * x_ref[...] + f, g_ref[...], b_ref[...])
    o_ref[...] = y
    ob_ref[...] = y.astype(BF16)


def _combine_ln(x, y_sorted, pos, wgt, g, b, bm):
    n, d = x.shape
    n_steps = n // bm
    pos3 = pos.reshape(n_steps, 1, 2 * bm)
    vmem = 2 * (bm * d * 4 + bm * LANES * 4 + bm * d * 4 + bm * d * 2) + 4 * bm * d * 4 + 3 * bm * d * 4 + (4 << 20)
    return pl.pallas_call(
        functools.partial(_combine_ln_kernel, bm=bm),
        grid=(n_steps,),
        in_specs=[pl.BlockSpec((1, 1, 2 * bm), lambda i: (i, 0, 0), memory_space=pltpu.SMEM),
                  pl.BlockSpec((1, 1, 2 * bm), lambda i: (jnp.minimum(i + 1, n_steps - 1), 0, 0),
                               memory_space=pltpu.SMEM),
                  pl.BlockSpec((bm, d), lambda i: (i, 0)),
                  pl.BlockSpec((bm, LANES), lambda i: (i, 0)),
                  pl.BlockSpec((1, d), lambda i: (0, 0)),
                  pl.BlockSpec((1, d), lambda i: (0, 0)),
                  pl.BlockSpec(memory_space=pl.ANY)],
        out_specs=[pl.BlockSpec((bm, d), lambda i: (i, 0)),
                   pl.BlockSpec((bm, d), lambda i: (i, 0))],
        out_shape=[jax.ShapeDtypeStruct((n, d), F32), jax.ShapeDtypeStruct((n, d), BF16)],
        scratch_shapes=[pltpu.VMEM((2, 2, bm, d), F32), pltpu.SemaphoreType.DMA((2,))],
        compiler_params=_cparams(vmem),
        name="moe_combine_ln",
    )(pos3, pos3, x, wgt, g.reshape(1, d), b.reshape(1, d), y_sorted)


def _moe_layer(x, xb, w_router, wg, wu, wd, ln_g, ln_b, tri):
    n = x.shape[0]
    n_tiles = (2 * n) // MOE_BM + N_EXPERTS
    idx, wgt, cnt = _router(xb, w_router, tri)
    counts = cnt[0, :N_EXPERTS].astype(I32)
    tiles_per = (counts + MOE_BM - 1) // MOE_BM
    tile_end = jnp.cumsum(tiles_per)
    offsets = (tile_end - tiles_per) * MOE_BM
    tile_id = jnp.arange(n_tiles, dtype=I32)
    last_tile = tile_end[-1] - 1
    tile_row = jnp.minimum(tile_id, last_tile)
    tile_expert = jnp.minimum(jnp.sum(tile_row[:, None] >= tile_end[None, :], axis=1), N_EXPERTS - 1).astype(I32)
    order = jnp.argsort(idx[:, 0:2].reshape(-1), stable=True).astype(I32)
    group_start = jnp.cumsum(counts) - counts
    rank = jnp.arange(MOE_BM, dtype=I32)[None, :] + ((tile_row - (tile_end - tiles_per)[tile_expert]) * MOE_BM)[:, None]
    valid = (rank < counts[tile_expert][:, None]) & (tile_id <= last_tile)[:, None]
    slot = jnp.clip(group_start[tile_expert][:, None] + rank, 0, 2 * n - 1)
    src = jnp.where(valid, order[slot] // 2, 0).reshape(-1).astype(I32)
    expert_ids = jnp.arange(N_EXPERTS, dtype=I32)
    off1 = jnp.sum(jnp.where(idx[:, 0:1] == expert_ids[None, :], offsets[None, :], 0), axis=1)
    off2 = jnp.sum(jnp.where(idx[:, 1:2] == expert_ids[None, :], offsets[None, :], 0), axis=1)
    pos = jnp.stack([off1 + idx[:, 2], off2 + idx[:, 3]], axis=1).astype(I32)

    xs = _dispatch(x, src, MOE_BM)
    n_used = tile_end[-1:].astype(I32)
    hdn = _moe_swiglu(xs, wg, wu, tile_expert, tile_row, n_used, bn=512)
    y_sorted = _moe_down(hdn, wd, tile_expert, tile_row, n_used, bn=1024)
    return _combine_ln(x, y_sorted, pos, wgt, ln_g, ln_b, COMBINE_BM)


def _rope_tables(dim):
    seg = dim // 2
    inv = ROPE_THETA ** (-jnp.arange(0, seg, 2, dtype=F32) / seg)
    t = jnp.arange(SEQ_LEN)
    row = (t // GRID_W).astype(F32)
    col = (t % GRID_W).astype(F32)
    ar = row[:, None] * inv[None, :]
    ac = col[:, None] * inv[None, :]
    ar = jnp.concatenate([ar, ar], axis=-1)
    ac = jnp.concatenate([ac, ac], axis=-1)
    cos = jnp.concatenate([jnp.cos(ar), jnp.cos(ac)], axis=-1)
    sin = jnp.concatenate([jnp.sin(ar), jnp.sin(ac)], axis=-1)
    half = seg // 2
    sign = jnp.where((jnp.arange(dim) % seg) < half, -1.0, 1.0).astype(F32)
    sin = sin * sign[None, :]
    pad = LANES - dim
    if pad:
        cos = jnp.concatenate([cos, jnp.zeros((SEQ_LEN, pad), F32)], axis=-1)
        sin = jnp.concatenate([sin, jnp.zeros((SEQ_LEN, pad), F32)], axis=-1)
    return cos, sin


def kernel(x_prompt, x_sample, w_in, gate_bias, q_a_norm, w_q_b, kv_a_norm, w_kv_b, q_norm, k_norm, w_br_a,
           w_br_b, w_o, ln1_g, ln1_b, ffn_w_gate, ffn_w_up, ffn_w_down, moe_w_router, moe_w_gate, moe_w_up,
           moe_w_down, ln2_g, ln2_b):
    n_prompt, n_sample = x_prompt.shape[0], x_sample.shape[0]
    n_batch = n_prompt + n_sample
    x = jnp.concatenate([x_prompt, x_sample], axis=0).reshape(n_batch * SEQ_LEN, D_MODEL)
    xb = x.astype(BF16)

    cos_a, sin_a = _rope_tables(ROPE_A)
    cos_b, sin_b = _rope_tables(HD_B)
    tri = jnp.tril(jnp.ones((ROUTE_BM, ROUTE_BM), BF16), k=-1)

    c0 = Q_LORA
    c1 = c0 + KV_LORA
    c2 = c1 + ROPE_A
    c3 = c2 + H_B * HD_B
    c4 = c3 + KV_B * HD_B
    c5 = c4 + KV_B * HD_B

    for l in range(DEPTH):
        wl = w_in[l]
        w_lat = wl[:, :c1 + LANES].astype(BF16)
        w_rest = wl[:, c2:].astype(BF16)
        wq = w_q_b[l].reshape(Q_LORA, H_A, NOPE_A + ROPE_A)
        wq = jnp.pad(wq, ((0, 0), (0, 0), (0, HEAD_A_PAD - NOPE_A - ROPE_A)))
        wq = wq.reshape(Q_LORA, H_A * HEAD_A_PAD).astype(BF16)
        w_kvb = w_kv_b[l].astype(BF16)
        g_qk = jnp.concatenate([jnp.tile(q_norm[l], H_B), jnp.tile(k_norm[l], KV_B)]).reshape(1, -1)
        bias = gate_bias[l].reshape(1, 2 * D_MODEL)

        q_lat = _mm_rms(xb, w_lat, q_a_norm[l], bm=1024, name="q_latent")
        kv_lat = _mm_rms(xb, w_lat, kv_a_norm[l], bm=1024, name="kv_latent", col0=c0 // KV_LORA)
        k_rope = _mm_rope(xb, w_lat, cos_a, sin_a, bm=1024, bn=LANES, half=ROPE_A // 4, rope_groups=1,
                          name="k_rope", n=LANES, col0=c1 // LANES)
        qk_b = _mm_headnorm_rope(xb, w_rest, g_qk, cos_b, sin_b, bm=1024, bn=512, half=HD_B // 4, name="qk_b")
        v_b = _mm_plain(xb, w_rest, BF16, bm=1024, bn=512, name="v_b", n=c5 - c4, col0=(c4 - c2) // 512)
        gates = _mm_bias_sigmoid(xb, w_rest, bias, bm=1024, bn=1024, name="gates", col0=(c5 - c2) // 1024)

        qa = _mm_rope(q_lat, wq, cos_a, sin_a, bm=1024, bn=1024, half=ROPE_A // 4, rope_groups=2, name="q_a")
        kv = _mm_plain(kv_lat, w_kvb, BF16, bm=1024, bn=1024, name="kv_a")
        o_a = _mla_attention(qa, kv, k_rope, n_batch, tq=512, tk=512)
        o_b = _gqa_attention(qk_b, v_b, n_batch, tq=128, tk=512)

        y = _mm_gated_merge(o_a, o_b, w_br_a[l].astype(BF16), w_br_b[l].astype(BF16), gates, bm=1024, bn=512,
                            name="gated_merge")
        mix = _mm_plain(y, w_o[l].astype(BF16), F32, bm=1024, bn=1024, name="w_o")
        x, xb = _ln_residual(x, mix, ln1_g[l], ln1_b[l], bm=256, name="ln1")

        i = l // 2
        if l % 2 == 0:
            pad = -D_FF_DENSE % DOWN_BK
            wg = jnp.pad(ffn_w_gate[i], ((0, 0), (0, pad))).astype(BF16)
            wu = jnp.pad(ffn_w_up[i], ((0, 0), (0, pad))).astype(BF16)
            wd = jnp.pad(ffn_w_down[i], ((0, pad), (0, 0))).astype(BF16).reshape(-1, DOWN_BK, D_MODEL)
            hdn = _mm_swiglu(xb, wg, wu, bm=1024, bn=512, name="ffn_swiglu")
            f = _mm_down(hdn, wd, bm=1024, bn=1024, name="ffn_down")
            x, xb = _ln_residual(x, f, ln2_g[l], ln2_b[l], bm=256, name="ln2")
        else:
            w_r = jnp.pad(moe_w_router[i], ((0, 0), (0, LANES - N_EXPERTS))).astype(BF16)
            x, xb = _moe_layer(x, xb, w_r, moe_w_gate[i].astype(BF16), moe_w_up[i].astype(BF16),
                               moe_w_down[i].astype(BF16), ln2_g[l], ln2_b[l], tri)

    y = x.reshape(n_batch, SEQ_LEN, D_MODEL)
    return y[:n_prompt], y[n_prompt:]
```

```python
import functools
import math

import jax
import jax.numpy as jnp
from jax import lax
from jax.experimental import pallas as pl
from jax.experimental.pallas import tpu as pltpu

F32 = jnp.float32
BF16 = jnp.bfloat16
I32 = jnp.int32

D_MODEL = 4096
SEQ_LEN = 4096
DEPTH = 4
GRID_W = 64
ROPE_THETA = 10000.0
NORM_EPS = 1e-6
LN_EPS = 1e-5
H_A, Q_LORA, KV_LORA, NOPE_A, ROPE_A, V_A = 16, 1024, 512, 128, 64, 128
H_B, KV_B, HD_B = 16, 4, 128
D_FF_DENSE = 11008
D_FF_EXPERT = 4096
N_EXPERTS = 8
DN_ALPHA = (2 * DEPTH) ** 0.25

LANES = 128
VMEM_BUDGET = 56 * 1024 * 1024

DOWN_BK = 2816
HEAD_A_PAD = 2 * LANES
MOE_BM = 512
ROUTE_BM = 1024
COMBINE_BM = 256
LOG2_E = math.log2(math.e)
ONES_ROWS = 16
ATTN_UNROLL = 8
EPILOGUE_ROWS = 256


def _cparams(vmem_bytes):
    return pltpu.CompilerParams(vmem_limit_bytes=int(min(VMEM_BUDGET, vmem_bytes)))


def _dot(a, b):
    return jnp.dot(a, b, preferred_element_type=F32)


def _rope_lanes(y, cos, sin_signed, half):
    lane = lax.broadcasted_iota(I32, y.shape, y.ndim - 1)
    lo = (lane % (2 * half)) < half
    partner = jnp.where(lo, pltpu.roll(y, LANES - half, y.ndim - 1), pltpu.roll(y, half, y.ndim - 1))
    return y * cos + partner * sin_signed


def _layernorm(z, g, b):
    mu = jnp.mean(z, axis=-1, keepdims=True)
    zc = z - mu
    var = jnp.mean(zc * zc, axis=-1, keepdims=True)
    return zc * lax.rsqrt(var + LN_EPS) * g + b


def _mm_plain_kernel(x_ref, w_ref, o_ref):
    o_ref[...] = _dot(x_ref[...], w_ref[...]).astype(o_ref.dtype)


def _mm_plain(x, w, out_dtype, bm, bn, name, n=None, col0=0):
    m, k = x.shape
    n = w.shape[1] if n is None else n
    ob = jnp.dtype(out_dtype).itemsize
    vmem = 2 * (bm * k * 2 + k * bn * 2 + bm * bn * ob) + bm * bn * 4 + (4 << 20)
    return pl.pallas_call(
        _mm_plain_kernel,
        grid=(m // bm, n // bn),
        in_specs=[pl.BlockSpec((bm, k), lambda i, j: (i, 0)),
                  pl.BlockSpec((k, bn), lambda i, j: (0, j + col0))],
        out_specs=pl.BlockSpec((bm, bn), lambda i, j: (i, j)),
        out_shape=jax.ShapeDtypeStruct((m, n), out_dtype),
        compiler_params=_cparams(vmem),
        name=name,
    )(x, w)


def _mm_rms_kernel(x_ref, w_ref, g_ref, o_ref):
    h = _dot(x_ref[...], w_ref[...])
    y = h * lax.rsqrt(jnp.mean(h * h, axis=-1, keepdims=True) + NORM_EPS)
    o_ref[...] = (y * g_ref[...]).astype(o_ref.dtype)


def _mm_rms(x, w, g, bm, name, col0=0):
    m, k = x.shape
    n = g.shape[0]
    vmem = 2 * (bm * k * 2 + k * n * 2 + bm * n * 2) + 2 * bm * n * 4 + (4 << 20)
    return pl.pallas_call(
        _mm_rms_kernel,
        grid=(m // bm,),
        in_specs=[pl.BlockSpec((bm, k), lambda i: (i, 0)),
                  pl.BlockSpec((k, n), lambda i: (0, col0)),
                  pl.BlockSpec((1, n), lambda i: (0, 0))],
        out_specs=pl.BlockSpec((bm, n), lambda i: (i, 0)),
        out_shape=jax.ShapeDtypeStruct((m, n), BF16),
        compiler_params=_cparams(vmem),
        name=name,
    )(x, w, g.reshape(1, n).astype(F32))


def _mm_rope_kernel(x_ref, w_ref, cos_ref, sin_ref, o_ref, *, half, rope_groups):
    h = _dot(x_ref[...], w_ref[...])
    n_groups = h.shape[1] // LANES
    for gi in range(n_groups):
        hg = h[:, gi * LANES:(gi + 1) * LANES]
        if gi % rope_groups == rope_groups - 1:
            hg = _rope_lanes(hg, cos_ref[...], sin_ref[...], half)
        o_ref[:, gi * LANES:(gi + 1) * LANES] = hg.astype(o_ref.dtype)


def _mm_rope(x, w, cos, sin, bm, bn, half, rope_groups, name, n=None, col0=0):
    m, k = x.shape
    n = w.shape[1] if n is None else n
    t_blocks = SEQ_LEN // bm
    vmem = 2 * (bm * k * 2 + k * bn * 2 + bm * bn * 2 + 2 * bm * LANES * 4) + 2 * bm * bn * 4 + (4 << 20)
    return pl.pallas_call(
        functools.partial(_mm_rope_kernel, half=half, rope_groups=rope_groups),
        grid=(m // bm, n // bn),
        in_specs=[pl.BlockSpec((bm, k), lambda i, j: (i, 0)),
                  pl.BlockSpec((k, bn), lambda i, j: (0, j + col0)),
                  pl.BlockSpec((bm, LANES), lambda i, j: (i % t_blocks, 0)),
                  pl.BlockSpec((bm, LANES), lambda i, j: (i % t_blocks, 0))],
        out_specs=pl.BlockSpec((bm, bn), lambda i, j: (i, j)),
        out_shape=jax.ShapeDtypeStruct((m, n), BF16),
        compiler_params=_cparams(vmem),
        name=name,
    )(x, w, cos, sin)


def _mm_headnorm_rope_kernel(x_ref, w_ref, g_ref, cos_ref, sin_ref, o_ref, *, half):
    for r0 in range(0, x_ref.shape[0], EPILOGUE_ROWS):
        rows = slice(r0, r0 + EPILOGUE_ROWS)
        h = _dot(x_ref[rows, :], w_ref[...])
        for gi in range(h.shape[1] // LANES):
            cols = slice(gi * LANES, (gi + 1) * LANES)
            hg = h[:, cols]
            y = hg * lax.rsqrt(jnp.mean(hg * hg, axis=-1, keepdims=True) + NORM_EPS)
            y = y * g_ref[:, cols]
            o_ref[rows, cols] = _rope_lanes(y, cos_ref[rows, :], sin_ref[rows, :], half).astype(o_ref.dtype)


def _mm_headnorm_rope(x, w, g, cos, sin, bm, bn, half, name):
    m, k = x.shape
    n = g.shape[1]
    t_blocks = SEQ_LEN // bm
    vmem = 2 * (bm * k * 2 + k * bn * 2 + bm * bn * 2 + 2 * bm * LANES * 4) + 2 * bm * bn * 4 + (4 << 20)
    return pl.pallas_call(
        functools.partial(_mm_headnorm_rope_kernel, half=half),
        grid=(m // bm, n // bn),
        in_specs=[pl.BlockSpec((bm, k), lambda i, j: (i, 0)),
                  pl.BlockSpec((k, bn), lambda i, j: (0, j)),
                  pl.BlockSpec((1, bn), lambda i, j: (0, j)),
                  pl.BlockSpec((bm, LANES), lambda i, j: (i % t_blocks, 0)),
                  pl.BlockSpec((bm, LANES), lambda i, j: (i % t_blocks, 0))],
        out_specs=pl.BlockSpec((bm, bn), lambda i, j: (i, j)),
        out_shape=jax.ShapeDtypeStruct((m, n), BF16),
        compiler_params=_cparams(vmem),
        name=name,
    )(x, w, g, cos, sin)


def _mm_bias_sigmoid_kernel(x_ref, w_ref, b_ref, o_ref):
    for r0 in range(0, x_ref.shape[0], EPILOGUE_ROWS):
        rows = slice(r0, r0 + EPILOGUE_ROWS)
        o_ref[rows, :] = jax.nn.sigmoid(_dot(x_ref[rows, :], w_ref[...]) + b_ref[...]).astype(o_ref.dtype)


def _mm_bias_sigmoid(x, w, b, bm, bn, name, col0=0):
    m, k = x.shape
    n = b.shape[1]
    vmem = 2 * (bm * k * 2 + k * bn * 2 + bm * bn * 2) + 2 * bm * bn * 4 + (4 << 20)
    return pl.pallas_call(
        _mm_bias_sigmoid_kernel,
        grid=(m // bm, n // bn),
        in_specs=[pl.BlockSpec((bm, k), lambda i, j: (i, 0)),
                  pl.BlockSpec((k, bn), lambda i, j: (0, j + col0)),
                  pl.BlockSpec((1, bn), lambda i, j: (0, j))],
        out_specs=pl.BlockSpec((bm, bn), lambda i, j: (i, j)),
        out_shape=jax.ShapeDtypeStruct((m, n), BF16),
        compiler_params=_cparams(vmem),
        name=name,
    )(x, w, b)


def _mm_gated_merge_kernel(oa_ref, ob_ref, wa_ref, wb_ref, ga_ref, gb_ref, o_ref):
    for r0 in range(0, oa_ref.shape[0], EPILOGUE_ROWS):
        rows = slice(r0, r0 + EPILOGUE_ROWS)
        pa = _dot(oa_ref[rows, :], wa_ref[...])
        pb = _dot(ob_ref[rows, :], wb_ref[...])
        o_ref[rows, :] = (ga_ref[rows, :].astype(F32) * pa + gb_ref[rows, :].astype(F32) * pb).astype(o_ref.dtype)


def _mm_gated_merge(oa, ob, wa, wb, gates, bm, bn, name):
    m, k = oa.shape
    n = wa.shape[1]
    nb = n // bn
    vmem = 2 * (2 * bm * k * 2 + 2 * k * bn * 2 + 3 * bm * bn * 2) + 3 * bm * bn * 4 + (4 << 20)
    return pl.pallas_call(
        _mm_gated_merge_kernel,
        grid=(m // bm, nb),
        in_specs=[pl.BlockSpec((bm, k), lambda i, j: (i, 0)),
                  pl.BlockSpec((bm, k), lambda i, j: (i, 0)),
                  pl.BlockSpec((k, bn), lambda i, j: (0, j)),
                  pl.BlockSpec((k, bn), lambda i, j: (0, j)),
                  pl.BlockSpec((bm, bn), lambda i, j: (i, j)),
                  pl.BlockSpec((bm, bn), lambda i, j: (i, j + nb))],
        out_specs=pl.BlockSpec((bm, bn), lambda i, j: (i, j)),
        out_shape=jax.ShapeDtypeStruct((m, n), BF16),
        compiler_params=_cparams(vmem),
        name=name,
    )(oa, ob, wa, wb, gates, gates)


def _mm_swiglu_kernel(x_ref, wg_ref, wu_ref, o_ref):
    x = x_ref[...]
    o_ref[...] = (jax.nn.silu(_dot(x, wg_ref[...])) * _dot(x, wu_ref[...])).astype(o_ref.dtype)


def _mm_swiglu(x, wg, wu, bm, bn, name):
    m, k = x.shape
    n = wg.shape[1]
    vmem = 2 * (bm * k * 2 + 2 * k * bn * 2 + bm * bn * 2) + 3 * bm * bn * 4 + (4 << 20)
    return pl.pallas_call(
        _mm_swiglu_kernel,
        grid=(m // bm, n // bn),
        in_specs=[pl.BlockSpec((bm, k), lambda i, j: (i, 0)),
                  pl.BlockSpec((k, bn), lambda i, j: (0, j)),
                  pl.BlockSpec((k, bn), lambda i, j: (0, j))],
        out_specs=pl.BlockSpec((bm, bn), lambda i, j: (i, j)),
        out_shape=jax.ShapeDtypeStruct((m, n), BF16),
        compiler_params=_cparams(vmem),
        name=name,
    )(x, wg, wu)


def _mm_down_kernel(h_ref, w_ref, o_ref, acc_ref, *, n_k):
    kk = pl.program_id(2)

    @pl.when(kk == 0)
    def _():
        acc_ref[...] = jnp.zeros_like(acc_ref)

    acc_ref[...] += _dot(h_ref[...], w_ref[...])

    @pl.when(kk == n_k - 1)
    def _():
        o_ref[...] = acc_ref[...]


def _mm_down(h, w, bm, bn, name):
    m = h.shape[0]
    n_k, bk, n = w.shape
    vmem = 2 * (bm * bk * 2 + bk * bn * 2 + bm * bn * 4) + 3 * bm * bn * 4 + (4 << 20)
    return pl.pallas_call(
        functools.partial(_mm_down_kernel, n_k=n_k),
        grid=(m // bm, n // bn, n_k),
        in_specs=[pl.BlockSpec((bm, bk), lambda i, j, k: (i, k)),
                  pl.BlockSpec((None, bk, bn), lambda i, j, k: (k, 0, j))],
        out_specs=pl.BlockSpec((bm, bn), lambda i, j, k: (i, j)),
        out_shape=jax.ShapeDtypeStruct((m, n), F32),
        scratch_shapes=[pltpu.VMEM((bm, bn), F32)],
        compiler_params=_cparams(vmem),
        name=name,
    )(h, w)


def _ln_kernel(x_ref, f_ref, g_ref, b_ref, o_ref, ob_ref):
    y = _layernorm(DN_ALPHA * x_ref[...] + f_ref[...], g_ref[...], b_ref[...])
    o_ref[...] = y
    ob_ref[...] = y.astype(BF16)


def _ln_residual(x, f, g, b, bm, name):
    m, d = x.shape
    vmem = 2 * (2 * bm * d * 4 + bm * d * 4 + bm * d * 2) + 3 * bm * d * 4 + (4 << 20)
    return pl.pallas_call(
        _ln_kernel,
        grid=(m // bm,),
        in_specs=[pl.BlockSpec((bm, d), lambda i: (i, 0)),
                  pl.BlockSpec((bm, d), lambda i: (i, 0)),
                  pl.BlockSpec((1, d), lambda i: (0, 0)),
                  pl.BlockSpec((1, d), lambda i: (0, 0))],
        out_specs=[pl.BlockSpec((bm, d), lambda i: (i, 0)),
                   pl.BlockSpec((bm, d), lambda i: (i, 0))],
        out_shape=[jax.ShapeDtypeStruct((m, d), F32), jax.ShapeDtypeStruct((m, d), BF16)],
        compiler_params=_cparams(vmem),
        name=name,
    )(x, f, g.reshape(1, d), b.reshape(1, d))


def _attention_step(q, k_ref, vt_ref, s_cur, s_prev, m_cur, m_prev, acc_ref, exp2_scale, tk):
    n_chunks = s_cur.shape[0]
    assert n_chunks % ATTN_UNROLL == 0
    dv = acc_ref.shape[0] - ONES_ROWS
    m_old = m_prev[...]
    acc_ref[...] = jnp.zeros_like(acc_ref)

    def body(gi, m):
        for u in range(ATTN_UNROLL):
            c = gi * ATTN_UNROLL + u
            kc = k_ref[pl.ds(pl.multiple_of(c * tk, tk), tk), :]
            s = lax.dot_general(kc, q, (((1,), (1,)), ((), ())), preferred_element_type=F32)
            s_cur[c] = s
            m = jnp.maximum(m, jnp.max(s, axis=0, keepdims=True))
            p = jnp.exp2((s_prev[c] - m_old) * exp2_scale)
            acc_ref[...] += _dot(vt_ref[c], p.astype(BF16))
        return m

    m_cur[...] = lax.fori_loop(0, n_chunks // ATTN_UNROLL, body, jnp.full((1, q.shape[0]), -jnp.inf, F32))
    return acc_ref[:dv, :] / acc_ref[dv:dv + 1, :]


def _pipelined_steps(load_q, k_ref, vt_ref, bufs, acc_ref, exp2_scale, tk, emit):
    s0, s1, m0, m1 = bufs
    step = pl.program_id(2)

    @pl.when(step % 2 == 0)
    def _():
        emit(_attention_step(load_q(), k_ref, vt_ref, s0, s1, m0, m1, acc_ref, exp2_scale, tk))

    @pl.when(step % 2 == 1)
    def _():
        emit(_attention_step(load_q(), k_ref, vt_ref, s1, s0, m1, m0, acc_ref, exp2_scale, tk))


def _start_sequence(v_ref, vt_ref, bufs, tk):
    _, s1, _, m1 = bufs
    dv = v_ref.shape[1]
    for c in range(vt_ref.shape[0]):
        vt_ref[c, :dv, :] = v_ref[c * tk:(c + 1) * tk, :].astype(F32).T.astype(BF16)
        vt_ref[c, dv:, :] = jnp.ones((ONES_ROWS, tk), BF16)
    s1[...] = jnp.zeros_like(s1)
    m1[...] = jnp.zeros_like(m1)


def _attention_scratch(dv, m_cols, tk):
    n_chunks = SEQ_LEN // tk
    return [pltpu.VMEM((n_chunks, dv + ONES_ROWS, tk), BF16),
            pltpu.VMEM((n_chunks, tk, m_cols), F32), pltpu.VMEM((n_chunks, tk, m_cols), F32),
            pltpu.VMEM((1, m_cols), F32), pltpu.VMEM((1, m_cols), F32),
            pltpu.VMEM((dv + ONES_ROWS, m_cols), F32)]


def _gqa_kernel(q_ref, k_ref, v_ref, o_ref, vt_ref, s0, s1, m0, m1, acc_ref, *, exp2_scale, tk, group):
    tq = q_ref.shape[0]
    bufs = (s0, s1, m0, m1)

    @pl.when(pl.program_id(2) == 0)
    def _():
        _start_sequence(v_ref, vt_ref, bufs, tk)

    def load_q():
        return jnp.concatenate([q_ref[:, h * HD_B:(h + 1) * HD_B] for h in range(group)], axis=0)

    def emit(ot):
        for h in range(group):
            o_ref[:, h * HD_B:(h + 1) * HD_B] = ot[:, h * tq:(h + 1) * tq].T.astype(o_ref.dtype)

    _pipelined_steps(load_q, k_ref, vt_ref, bufs, acc_ref, exp2_scale, tk, emit)


def _gqa_attention(qk, v, n_batch, tq, tk):
    n = qk.shape[0]
    group = H_B // KV_B
    qt = SEQ_LEN // tq
    m_cols = group * tq
    vmem = (2 * (2 * tq * group * HD_B * 2 + 2 * SEQ_LEN * HD_B * 2) + SEQ_LEN * HD_B * 2
            + 2 * m_cols * SEQ_LEN * 4 + HD_B * m_cols * 4 + (16 << 20))
    return pl.pallas_call(
        functools.partial(_gqa_kernel, exp2_scale=HD_B ** -0.5 * LOG2_E, tk=tk, group=group),
        grid=(n_batch, KV_B, qt + 1),
        in_specs=[pl.BlockSpec((tq, group * HD_B), lambda b, g, i: (b * qt + jnp.minimum(i, qt - 1), g)),
                  pl.BlockSpec((SEQ_LEN, HD_B), lambda b, g, i: (b, H_B + g)),
                  pl.BlockSpec((SEQ_LEN, HD_B), lambda b, g, i: (b, g))],
        out_specs=pl.BlockSpec((tq, group * HD_B), lambda b, g, i: (b * qt + jnp.maximum(i - 1, 0), g)),
        out_shape=jax.ShapeDtypeStruct((n, H_B * HD_B), BF16),
        scratch_shapes=_attention_scratch(HD_B, m_cols, tk),
        compiler_params=_cparams(vmem),
        name="gqa_attention",
    )(qk, qk, v)


def _mla_kernel(q_ref, kn_ref, kr_ref, v_ref, o_ref, kfull_ref, vt_ref, s0, s1, m0, m1, acc_ref, *,
                exp2_scale, tk):
    bufs = (s0, s1, m0, m1)

    @pl.when(pl.program_id(2) == 0)
    def _():
        kfull_ref[:, :NOPE_A] = kn_ref[...]
        kfull_ref[:, NOPE_A:] = kr_ref[...]
        _start_sequence(v_ref, vt_ref, bufs, tk)

    def emit(ot):
        o_ref[...] = ot.T.astype(o_ref.dtype)

    _pipelined_steps(lambda: q_ref[...], kfull_ref, vt_ref, bufs, acc_ref, exp2_scale, tk, emit)


def _mla_attention(qa, kv, kr, n_batch, tq, tk):
    n = qa.shape[0]
    qt = SEQ_LEN // tq
    vmem = (2 * (tq * HEAD_A_PAD * 2 + 3 * SEQ_LEN * LANES * 2 + tq * V_A * 2)
            + SEQ_LEN * HEAD_A_PAD * 2 + SEQ_LEN * V_A * 2 + 2 * tq * SEQ_LEN * 4 + V_A * tq * 4 + (16 << 20))
    return pl.pallas_call(
        functools.partial(_mla_kernel, exp2_scale=(NOPE_A + ROPE_A) ** -0.5 * LOG2_E, tk=tk),
        grid=(n_batch, H_A, qt + 1),
        in_specs=[pl.BlockSpec((tq, HEAD_A_PAD), lambda b, h, i: (b * qt + jnp.minimum(i, qt - 1), h)),
                  pl.BlockSpec((SEQ_LEN, NOPE_A), lambda b, h, i: (b, 2 * h)),
                  pl.BlockSpec((SEQ_LEN, LANES), lambda b, h, i: (b, 0)),
                  pl.BlockSpec((SEQ_LEN, V_A), lambda b, h, i: (b, 2 * h + 1))],
        out_specs=pl.BlockSpec((tq, V_A), lambda b, h, i: (b * qt + jnp.maximum(i - 1, 0), h)),
        out_shape=jax.ShapeDtypeStruct((n, H_A * V_A), BF16),
        scratch_shapes=[pltpu.VMEM((SEQ_LEN, HEAD_A_PAD), BF16)] + _attention_scratch(V_A, tq, tk),
        compiler_params=_cparams(vmem),
        name="mla_attention",
    )(qa, kv, kr, kv)


def _router_kernel(x_ref, w_ref, tri_ref, idx_ref, wgt_ref, cnt_ref, run_ref):
    @pl.when(pl.program_id(0) == 0)
    def _():
        run_ref[...] = jnp.zeros_like(run_ref)

    logits = _dot(x_ref[...], w_ref[...])
    lane = lax.broadcasted_iota(I32, logits.shape, 1)
    neg = jnp.float32(-jnp.inf)
    logits = jnp.where(lane < N_EXPERTS, logits, neg)
    v1 = jnp.max(logits, axis=-1, keepdims=True)
    i1 = jnp.min(jnp.where(logits == v1, lane, LANES), axis=-1, keepdims=True)
    rest = jnp.where(lane == i1, neg, logits)
    v2 = jnp.max(rest, axis=-1, keepdims=True)
    i2 = jnp.min(jnp.where(rest == v2, lane, LANES), axis=-1, keepdims=True)
    e2 = jnp.exp(v2 - v1)
    denom = 1.0 + e2
    wgt_ref[...] = jnp.where(lane == 0, 1.0 / denom, 0.0) + jnp.where(lane == 1, e2 / denom, 0.0)

    hit1 = lane == i1
    hit2 = lane == i2
    onehot = jnp.where(hit1 | hit2, 1.0, 0.0)
    before = _dot(tri_ref[...], onehot.astype(BF16)) + run_ref[...]
    r1 = jnp.sum(jnp.where(hit1, before, 0.0), axis=-1, keepdims=True)
    r2 = jnp.sum(jnp.where(hit2, before, 0.0), axis=-1, keepdims=True)
    idx_ref[...] = (jnp.where(lane == 0, i1, 0) + jnp.where(lane == 1, i2, 0)
                    + jnp.where(lane == 2, r1.astype(I32), 0) + jnp.where(lane == 3, r2.astype(I32), 0))
    run_ref[...] += jnp.sum(onehot, axis=0, keepdims=True)
    cnt_ref[...] = jnp.broadcast_to(run_ref[...], cnt_ref.shape)


def _router(x, w_router_pad, tri):
    m, k = x.shape
    bm = tri.shape[0]
    vmem = 2 * (bm * k * 2 + k * LANES * 2 + bm * bm * 2 + 2 * bm * LANES * 4) + (12 << 20)
    return pl.pallas_call(
        _router_kernel,
        grid=(m // bm,),
        in_specs=[pl.BlockSpec((bm, k), lambda i: (i, 0)),
                  pl.BlockSpec((k, LANES), lambda i: (0, 0)),
                  pl.BlockSpec((bm, bm), lambda i: (0, 0))],
        out_specs=[pl.BlockSpec((bm, LANES), lambda i: (i, 0)),
                   pl.BlockSpec((bm, LANES), lambda i: (i, 0)),
                   pl.BlockSpec((8, LANES), lambda i: (0, 0))],
        out_shape=[jax.ShapeDtypeStruct((m, LANES), I32),
                   jax.ShapeDtypeStruct((m, LANES), F32),
                   jax.ShapeDtypeStruct((8, LANES), F32)],
        scratch_shapes=[pltpu.VMEM((1, LANES), F32)],
        compiler_params=_cparams(vmem),
        name="moe_router",
    )(x, w_router_pad, tri)


def _row_copy(src, src_row, dst, dst_row, sem):
    return pltpu.make_async_copy(src.at[pl.ds(src_row, 1)], dst.at[pl.ds(dst_row, 1)], sem)


def _gather_rows(src_hbm, idx_ref, n_rows, stride, offset, dst, sem, wait):
    def body(r, carry):
        copy = _row_copy(src_hbm, idx_ref[0, 0, stride * r + offset], dst, r, sem)
        if wait:
            copy.wait()
        else:
            copy.start()
        return carry

    lax.fori_loop(0, n_rows, body, 0, unroll=8)


def _dispatch_kernel(src_ref, nxt_ref, x_hbm, o_ref, buf_ref, sem, *, bm):
    step = pl.program_id(0)
    slot = step % 2

    @pl.when(step == 0)
    def _():
        _gather_rows(x_hbm, src_ref, bm, 1, 0, buf_ref.at[0], sem.at[0], wait=False)

    @pl.when(step + 1 < pl.num_programs(0))
    def _():
        _gather_rows(x_hbm, nxt_ref, bm, 1, 0, buf_ref.at[1 - slot], sem.at[1 - slot], wait=False)

    _gather_rows(x_hbm, src_ref, bm, 1, 0, buf_ref.at[slot], sem.at[slot], wait=True)
    o_ref[...] = buf_ref[slot].astype(o_ref.dtype)


def _dispatch(x, src, bm):
    n, d = x.shape
    p = src.shape[0]
    n_steps = p // bm
    src3 = src.reshape(n_steps, 1, bm)
    vmem = 2 * bm * d * 2 + 3 * bm * d * 4 + (4 << 20)
    return pl.pallas_call(
        functools.partial(_dispatch_kernel, bm=bm),
        grid=(n_steps,),
        in_specs=[pl.BlockSpec((1, 1, bm), lambda i: (i, 0, 0), memory_space=pltpu.SMEM),
                  pl.BlockSpec((1, 1, bm), lambda i: (jnp.minimum(i + 1, n_steps - 1), 0, 0),
                               memory_space=pltpu.SMEM),
                  pl.BlockSpec(memory_space=pl.ANY)],
        out_specs=pl.BlockSpec((bm, d), lambda i: (i, 0)),
        out_shape=jax.ShapeDtypeStruct((p, d), BF16),
        scratch_shapes=[pltpu.VMEM((2, bm, d), F32), pltpu.SemaphoreType.DMA((2,))],
        compiler_params=_cparams(vmem),
        name="moe_dispatch",
    )(src3, src3, x)


def _grouped_col(i, j, used, n_col):
    return jnp.where(i < used[0], j, n_col - 1)


def _moe_swiglu_kernel(te_ref, tr_ref, used_ref, x_ref, wg_ref, wu_ref, o_ref):
    live = pl.program_id(0) < used_ref[0]

    @pl.when(live)
    def _():
        x = x_ref[...]
        o_ref[...] = (jax.nn.silu(_dot(x, wg_ref[...])) * _dot(x, wu_ref[...])).astype(o_ref.dtype)

    @pl.when(jnp.logical_not(live))
    def _():
        o_ref[...] = jnp.zeros_like(o_ref)


def _moe_swiglu(xs, wg, wu, tile_expert, tile_row, n_used, bn):
    p, k = xs.shape
    n = wg.shape[2]
    bm = MOE_BM
    nc = n // bn
    vmem = 2 * (bm * k * 2 + 2 * k * bn * 2 + bm * bn * 2) + 3 * bm * bn * 4 + (4 << 20)
    return pl.pallas_call(
        _moe_swiglu_kernel,
        grid_spec=pltpu.PrefetchScalarGridSpec(
            num_scalar_prefetch=3,
            grid=(p // bm, nc),
            in_specs=[pl.BlockSpec((bm, k), lambda i, j, te, tr, u: (tr[i], 0)),
                      pl.BlockSpec((None, k, bn), lambda i, j, te, tr, u: (te[i], 0, _grouped_col(i, j, u, nc))),
                      pl.BlockSpec((None, k, bn), lambda i, j, te, tr, u: (te[i], 0, _grouped_col(i, j, u, nc)))],
            out_specs=pl.BlockSpec((bm, bn), lambda i, j, te, tr, u: (i, j)),
        ),
        out_shape=jax.ShapeDtypeStruct((p, n), BF16),
        compiler_params=_cparams(vmem),
        name="moe_swiglu",
    )(tile_expert, tile_row, n_used, xs, wg, wu)


def _moe_down_kernel(te_ref, tr_ref, used_ref, h_ref, w_ref, o_ref):
    live = pl.program_id(0) < used_ref[0]

    @pl.when(live)
    def _():
        o_ref[...] = _dot(h_ref[...], w_ref[...])

    @pl.when(jnp.logical_not(live))
    def _():
        o_ref[...] = jnp.zeros_like(o_ref)


def _moe_down(h, wd, tile_expert, tile_row, n_used, bn):
    p, k = h.shape
    n = wd.shape[2]
    bm = MOE_BM
    nc = n // bn
    vmem = 2 * (bm * k * 2 + k * bn * 2 + bm * bn * 4) + bm * bn * 4 + (4 << 20)
    return pl.pallas_call(
        _moe_down_kernel,
        grid_spec=pltpu.PrefetchScalarGridSpec(
            num_scalar_prefetch=3,
            grid=(p // bm, nc),
            in_specs=[pl.BlockSpec((bm, k), lambda i, j, te, tr, u: (tr[i], 0)),
                      pl.BlockSpec((None, k, bn), lambda i, j, te, tr, u: (te[i], 0, _grouped_col(i, j, u, nc)))],
            out_specs=pl.BlockSpec((bm, bn), lambda i, j, te, tr, u: (i, j)),
        ),
        out_shape=jax.ShapeDtypeStruct((p, n), F32),
        compiler_params=_cparams(vmem),
        name="moe_down",
    )(tile_expert, tile_row, n_used, h, wd)


def _combine_ln_kernel(pos_ref, nxt_ref, x_ref, wgt_ref, g_ref, b_ref, y_hbm, o_ref, ob_ref, ybuf_ref, sem, *, bm):
    step = pl.program_id(0)
    slot = step % 2

    def gather(idx_ref, s, wait):
        for k in range(2):
            _gather_rows(y_hbm, idx_ref, bm, 2, k, ybuf_ref.at[s, k], sem.at[s], wait)

    @pl.when(step == 0)
    def _():
        gather(pos_ref, 0, wait=False)

    @pl.when(step + 1 < pl.num_programs(0))
    def _():
        gather(nxt_ref, 1 - slot, wait=False)

    gather(pos_ref, slot, wait=True)
    wgt = wgt_ref[...]
    f = wgt[:, 0:1] * ybuf_ref[slot, 0] + wgt[:, 1:2] * ybuf_ref[slot, 1]
    y = _layernorm(DN_ALPHA * x_ref[...] + f, g_ref[...], b_ref[...])
    o_ref[...] = y
    ob_ref[...] = y.astype(BF16)


def _combine_ln(x, y_sorted, pos, wgt, g, b, bm):
    n, d = x.shape
    n_steps = n // bm
    pos3 = pos.reshape(n_steps, 1, 2 * bm)
    vmem = 2 * (bm * d * 4 + bm * LANES * 4 + bm * d * 4 + bm * d * 2) + 4 * bm * d * 4 + 3 * bm * d * 4 + (4 << 20)
    return pl.pallas_call(
        functools.partial(_combine_ln_kernel, bm=bm),
        grid=(n_steps,),
        in_specs=[pl.BlockSpec((1, 1, 2 * bm), lambda i: (i, 0, 0), memory_space=pltpu.SMEM),
                  pl.BlockSpec((1, 1, 2 * bm), lambda i: (jnp.minimum(i + 1, n_steps - 1), 0, 0),
                               memory_space=pltpu.SMEM),
                  pl.BlockSpec((bm, d), lambda i: (i, 0)),
                  pl.BlockSpec((bm, LANES), lambda i: (i, 0)),
                  pl.BlockSpec((1, d), lambda i: (0, 0)),
                  pl.BlockSpec((1, d), lambda i: (0, 0)),
                  pl.BlockSpec(memory_space=pl.ANY)],
        out_specs=[pl.BlockSpec((bm, d), lambda i: (i, 0)),
                   pl.BlockSpec((bm, d), lambda i: (i, 0))],
        out_shape=[jax.ShapeDtypeStruct((n, d), F32), jax.ShapeDtypeStruct((n, d), BF16)],
        scratch_shapes=[pltpu.VMEM((2, 2, bm, d), F32), pltpu.SemaphoreType.DMA((2,))],
        compiler_params=_cparams(vmem),
        name="moe_combine_ln",
    )(pos3, pos3, x, wgt, g.reshape(1, d), b.reshape(1, d), y_sorted)


def _moe_layer(x, xb, w_router, wg, wu, wd, ln_g, ln_b, tri):
    n = x.shape[0]
    n_tiles = (2 * n) // MOE_BM + N_EXPERTS
    idx, wgt, cnt = _router(xb, w_router, tri)
    counts = cnt[0, :N_EXPERTS].astype(I32)
    tiles_per = (counts + MOE_BM - 1) // MOE_BM
    tile_end = jnp.cumsum(tiles_per)
    offsets = (tile_end - tiles_per) * MOE_BM
    tile_id = jnp.arange(n_tiles, dtype=I32)
    last_tile = tile_end[-1] - 1
    tile_row = jnp.minimum(tile_id, last_tile)
    tile_expert = jnp.minimum(jnp.sum(tile_row[:, None] >= tile_end[None, :], axis=1), N_EXPERTS - 1).astype(I32)
    order = jnp.argsort(idx[:, 0:2].reshape(-1), stable=True).astype(I32)
    group_start = jnp.cumsum(counts) - counts
    rank = jnp.arange(MOE_BM, dtype=I32)[None, :] + ((tile_row - (tile_end - tiles_per)[tile_expert]) * MOE_BM)[:, None]
    valid = (rank < counts[tile_expert][:, None]) & (tile_id <= last_tile)[:, None]
    slot = jnp.clip(group_start[tile_expert][:, None] + rank, 0, 2 * n - 1)
    src = jnp.where(valid, order[slot] // 2, 0).reshape(-1).astype(I32)
    expert_ids = jnp.arange(N_EXPERTS, dtype=I32)
    off1 = jnp.sum(jnp.where(idx[:, 0:1] == expert_ids[None, :], offsets[None, :], 0), axis=1)
    off2 = jnp.sum(jnp.where(idx[:, 1:2] == expert_ids[None, :], offsets[None, :], 0), axis=1)
    pos = jnp.stack([off1 + idx[:, 2], off2 + idx[:, 3]], axis=1).astype(I32)

    xs = _dispatch(x, src, MOE_BM)
    n_used = tile_end[-1:].astype(I32)
    hdn = _moe_swiglu(xs, wg, wu, tile_expert, tile_row, n_used, bn=512)
    y_sorted = _moe_down(hdn, wd, tile_expert, tile_row, n_used, bn=1024)
    return _combine_ln(x, y_sorted, pos, wgt, ln_g, ln_b, COMBINE_BM)


def _rope_tables(dim):
    seg = dim // 2
    inv = ROPE_THETA ** (-jnp.arange(0, seg, 2, dtype=F32) / seg)
    t = jnp.arange(SEQ_LEN)
    row = (t // GRID_W).astype(F32)
    col = (t % GRID_W).astype(F32)
    ar = row[:, None] * inv[None, :]
    ac = col[:, None] * inv[None, :]
    ar = jnp.concatenate([ar, ar], axis=-1)
    ac = jnp.concatenate([ac, ac], axis=-1)
    cos = jnp.concatenate([jnp.cos(ar), jnp.cos(ac)], axis=-1)
    sin = jnp.concatenate([jnp.sin(ar), jnp.sin(ac)], axis=-1)
    half = seg // 2
    sign = jnp.where((jnp.arange(dim) % seg) < half, -1.0, 1.0).astype(F32)
    sin = sin * sign[None, :]
    pad = LANES - dim
    if pad:
        cos = jnp.concatenate([cos, jnp.zeros((SEQ_LEN, pad), F32)], axis=-1)
        sin = jnp.concatenate([sin, jnp.zeros((SEQ_LEN, pad), F32)], axis=-1)
    return cos, sin


def kernel(x_prompt, x_sample, w_in, gate_bias, q_a_norm, w_q_b, kv_a_norm, w_kv_b, q_norm, k_norm, w_br_a,
           w_br_b, w_o, ln1_g, ln1_b, ffn_w_gate, ffn_w_up, ffn_w_down, moe_w_router, moe_w_gate, moe_w_up,
           moe_w_down, ln2_g, ln2_b):
    n_prompt, n_sample = x_prompt.shape[0], x_sample.shape[0]
    n_batch = n_prompt + n_sample
    x = jnp.concatenate([x_prompt, x_sample], axis=0).reshape(n_batch * SEQ_LEN, D_MODEL)
    xb = x.astype(BF16)

    cos_a, sin_a = _rope_tables(ROPE_A)
    cos_b, sin_b = _rope_tables(HD_B)
    tri = jnp.tril(jnp.ones((ROUTE_BM, ROUTE_BM), BF16), k=-1)

    c0 = Q_LORA
    c1 = c0 + KV_LORA
    c2 = c1 + ROPE_A
    c3 = c2 + H_B * HD_B
    c4 = c3 + KV_B * HD_B
    c5 = c4 + KV_B * HD_B

    for l in range(DEPTH):
        wl = w_in[l]
        w_lat = wl[:, :c1 + LANES].astype(BF16)
        w_rest = wl[:, c2:].astype(BF16)
        wq = w_q_b[l].reshape(Q_LORA, H_A, NOPE_A + ROPE_A)
        wq = jnp.pad(wq, ((0, 0), (0, 0), (0, HEAD_A_PAD - NOPE_A - ROPE_A)))
        wq = wq.reshape(Q_LORA, H_A * HEAD_A_PAD).astype(BF16)
        w_kvb = w_kv_b[l].astype(BF16)
        g_qk = jnp.concatenate([jnp.tile(q_norm[l], H_B), jnp.tile(k_norm[l], KV_B)]).reshape(1, -1)
        bias = gate_bias[l].reshape(1, 2 * D_MODEL)

        q_lat = _mm_rms(xb, w_lat, q_a_norm[l], bm=1024, name="q_latent")
        kv_lat = _mm_rms(xb, w_lat, kv_a_norm[l], bm=1024, name="kv_latent", col0=c0 // KV_LORA)
        k_rope = _mm_rope(xb, w_lat, cos_a, sin_a, bm=1024, bn=LANES, half=ROPE_A // 4, rope_groups=1,
                          name="k_rope", n=LANES, col0=c1 // LANES)
        qk_b = _mm_headnorm_rope(xb, w_rest, g_qk, cos_b, sin_b, bm=1024, bn=512, half=HD_B // 4, name="qk_b")
        v_b = _mm_plain(xb, w_rest, BF16, bm=1024, bn=512, name="v_b", n=c5 - c4, col0=(c4 - c2) // 512)
        gates = _mm_bias_sigmoid(xb, w_rest, bias, bm=1024, bn=1024, name="gates", col0=(c5 - c2) // 1024)

        qa = _mm_rope(q_lat, wq, cos_a, sin_a, bm=1024, bn=1024, half=ROPE_A // 4, rope_groups=2, name="q_a")
        kv = _mm_plain(kv_lat, w_kvb, BF16, bm=1024, bn=1024, name="kv_a")
        o_a = _mla_attention(qa, kv, k_rope, n_batch, tq=512, tk=512)
        o_b = _gqa_attention(qk_b, v_b, n_batch, tq=128, tk=512)

        y = _mm_gated_merge(o_a, o_b, w_br_a[l].astype(BF16), w_br_b[l].astype(BF16), gates, bm=1024, bn=512,
                            name="gated_merge")
        mix = _mm_plain(y, w_o[l].astype(BF16), F32, bm=1024, bn=1024, name="w_o")
        x, xb = _ln_residual(x, mix, ln1_g[l], ln1_b[l], bm=256, name="ln1")

        i = l // 2
        if l % 2 == 0:
            pad = -D_FF_DENSE % DOWN_BK
            wg = jnp.pad(ffn_w_gate[i], ((0, 0), (0, pad))).astype(BF16)
            wu = jnp.pad(ffn_w_up[i], ((0, 0), (0, pad))).astype(BF16)
            wd = jnp.pad(ffn_w_down[i], ((0, pad), (0, 0))).astype(BF16).reshape(-1, DOWN_BK, D_MODEL)
            hdn = _mm_swiglu(xb, wg, wu, bm=1024, bn=512, name="ffn_swiglu")
            f = _mm_down(hdn, wd, bm=1024, bn=1024, name="ffn_down")
            x, xb = _ln_residual(x, f, ln2_g[l], ln2_b[l], bm=256, name="ln2")
        else:
            w_r = jnp.pad(moe_w_router[i], ((0, 0), (0, LANES - N_EXPERTS))).astype(BF16)
            x, xb = _moe_layer(x, xb, w_r, moe_w_gate[i].astype(BF16), moe_w_up[i].astype(BF16),
                               moe_w_down[i].astype(BF16), ln2_g[l], ln2_b[l], tri)

    y = x.reshape(n_batch, SEQ_LEN, D_MODEL)
    return y[:n_prompt], y[n_prompt:]
```
